```python
import math
import jax, jax.numpy as jnp
from jax import lax
import numpy as np

D_MODEL = 1024
BATCH = 8
SEQ = 8192
DEPTH = 1
DEC_BATCH = 16
DEC_SEQ = 64
PAST_LEN = 4096

CHUNK = 64
Q_BLOCK = 128
NORM_EPS = 1e-6

SSM_D_INNER = 2 * D_MODEL
SSM_HEAD_DIM = 64
SSM_HEADS = SSM_D_INNER // SSM_HEAD_DIM
SSM_GROUPS = 4
SSM_HEADS_PER_GROUP = SSM_HEADS // SSM_GROUPS
SSM_D_STATE = 128
SSM_CONV = 4
SSM_CONV_DIM = SSM_D_INNER + 2 * SSM_GROUPS * SSM_D_STATE
DT_MIN = 1e-3
DT_MAX = 1e-1

ATT_HEADS = 8
ATT_HEAD_DIM = 64
ATT_QK_WIDTH = 2 * ATT_HEADS * ATT_HEAD_DIM
ATT_V_WIDTH = ATT_HEADS * 2 * ATT_HEAD_DIM
ROPE_THETA = 500000.0
ROT_DIM = ATT_HEAD_DIM // 4

D_FF = 2816
FFN_CONV = 3

IN_SIZES = (SSM_D_INNER, SSM_CONV_DIM, SSM_HEADS, ATT_QK_WIDTH, ATT_QK_WIDTH, ATT_V_WIDTH, D_MODEL, D_MODEL)
IN_WIDTH = sum(IN_SIZES)

kernel_name = "hybrid_ssd_diffattn_convffn_stream_step"


def _rmsnorm(x, w):
    xf = x.astype(jnp.float32)
    y = xf * lax.rsqrt(jnp.mean(xf * xf, axis=-1, keepdims=True) + NORM_EPS)
    return (y * w.astype(jnp.float32)).astype(x.dtype)


def _causal_dwconv(x_past, x_new, w, b):
    width = w.shape[0]
    xp = jnp.concatenate([x_past, x_new], axis=1)
    out = lax.conv_general_dilated(xp, w[:, None, :], window_strides=(1,), padding='VALID',
                                   dimension_numbers=('NWC', 'WIO', 'NWC'),
                                   feature_group_count=w.shape[1])
    return out + b, xp[:, -(width - 1):]


def _partial_rope(x, pos):
    half = ROT_DIM // 2
    inv_freq = jnp.power(ROPE_THETA, -jnp.arange(half, dtype=jnp.float32) * 2.0 / ROT_DIM)
    ang = pos.astype(jnp.float32)[:, None] * inv_freq[None, :]
    cos = jnp.cos(ang)[None, :, None, :]
    sin = jnp.sin(ang)[None, :, None, :]
    xf = x.astype(jnp.float32)
    x1 = xf[..., :half]
    x2 = xf[..., half:ROT_DIM]
    out = jnp.concatenate([x1 * cos - x2 * sin, x2 * cos + x1 * sin, xf[..., ROT_DIM:]], axis=-1)
    return out.astype(x.dtype)


def _diff_attn_block(q, k, v, q_pos, k_pos, lam):
    b, nq = q.shape[:2]
    s = jnp.einsum('bqhd,bkhd->bhqk', q, k, preferred_element_type=jnp.float32) * (ATT_HEAD_DIM ** -0.5)
    visible = (k_pos[None, :] // CHUNK) <= (q_pos[:, None] // CHUNK)
    s = jnp.where(visible[None, None], s, -jnp.inf)
    p = jax.nn.softmax(s, axis=-1).reshape(b, ATT_HEADS, 2, nq, -1)
    a = p[:, :, 0] - lam * p[:, :, 1]
    return jnp.einsum('bhqk,bkhe->bqhe', a.astype(v.dtype), v)


def _diff_attention(q, k, v, q_pos, k_pos, lam):
    b, t = q.shape[:2]
    if t <= Q_BLOCK:
        return _diff_attn_block(q, k, v, q_pos, k_pos, lam)
    nb = t // Q_BLOCK
    qb = jnp.swapaxes(q.reshape(b, nb, Q_BLOCK, 2 * ATT_HEADS, ATT_HEAD_DIM), 0, 1)
    pb = q_pos.reshape(nb, Q_BLOCK)
    ob = lax.map(lambda a: _diff_attn_block(a[0], k, v, a[1], k_pos, lam), (qb, pb))
    return jnp.swapaxes(ob, 0, 1).reshape(b, t, ATT_HEADS, 2 * ATT_HEAD_DIM)


def _ssd_scan(x, dt, a_neg, bmat, cmat, init_state, chunk_len):
    b, t = x.shape[:2]
    nc = t // chunk_len

    def to_chunks(z):
        return jnp.moveaxis(z.reshape((b, nc, chunk_len) + z.shape[2:]), 1, 0)

    causal = jnp.tril(jnp.ones((chunk_len, chunk_len), dtype=bool))

    def step(state, inp):
        xc, dtc, bc, cc = inp
        xc = xc.astype(jnp.float32)
        bc = bc.astype(jnp.float32)
        cc = cc.astype(jnp.float32)
        acum = jnp.cumsum(dtc * a_neg, axis=1)
        seg = acum[:, :, None] - acum[:, None, :]
        decay = jnp.exp(jnp.where(causal[None, :, :, None, None], seg, -jnp.inf))
        xdt = xc * dtc[..., None]
        cb = jnp.einsum('blgn,bsgn->blsg', cc, bc)
        y = jnp.einsum('blsg,blsge,bsgep->blgep', cb, decay, xdt)
        y = y + jnp.einsum('blgn,bgepn->blgep', cc, state) * jnp.exp(acum)[..., None]
        to_end = jnp.exp(acum[:, -1:] - acum)
        state = state * jnp.exp(acum[:, -1])[..., None, None] + jnp.einsum('blgn,blge,blgep->bgepn', bc, to_end, xdt)
        return state, y

    final, ys = lax.scan(step, init_state.astype(jnp.float32),
                         (to_chunks(x), to_chunks(dt), to_chunks(bmat), to_chunks(cmat)))
    y = jnp.moveaxis(ys, 0, 1).reshape(x.shape)
    return y, final


def _layer(x, k_past, v_past, ssm_past, conv_past, ffn_past, lambda_init, p):
    b, t, _ = x.shape
    past_len = k_past.shape[1]
    q_pos = past_len + jnp.arange(t, dtype=jnp.int32)
    k_pos = jnp.arange(past_len + t, dtype=jnp.int32)
    G, E, P, N = SSM_GROUPS, SSM_HEADS_PER_GROUP, SSM_HEAD_DIM, SSM_D_STATE

    xn = _rmsnorm(x, p['norm_mix_w'])
    proj = xn @ p['w_in']
    z, xbc, dt_raw, q, k, v, gate_ssm, gate_att = jnp.split(proj, np.cumsum(IN_SIZES)[:-1].tolist(), axis=-1)

    xbc_c, conv_new = _causal_dwconv(conv_past, xbc, p['ssm_conv_w'], p['ssm_conv_b'])
    xbc_c = jax.nn.silu(xbc_c)
    xs, bm, cm = jnp.split(xbc_c, [SSM_D_INNER, SSM_D_INNER + G * N], axis=-1)
    xs = xs.reshape(b, t, G, E, P)
    bm = bm.reshape(b, t, G, N)
    cm = cm.reshape(b, t, G, N)
    dt = jax.nn.softplus(dt_raw.astype(jnp.float32) + p['ssm_dt_bias'].astype(jnp.float32)).reshape(b, t, G, E)
    a_neg = -jnp.exp(p['ssm_a_log'].astype(jnp.float32)).reshape(G, E)
    chunk_len = min(CHUNK, t)
    y_ssm, ssm_new = _ssd_scan(xs, dt, a_neg, bm, cm, ssm_past.reshape(b, G, E, P, N), chunk_len)
    y_ssm = (y_ssm + p['ssm_d'].reshape(G, E)[..., None] * xs).astype(x.dtype).reshape(b, t, SSM_D_INNER)
    y_ssm = y_ssm * jax.nn.silu(z)
    y_ssm = _rmsnorm(y_ssm.reshape(b, t, G, -1), p['ssm_norm_w'].reshape(G, -1)).reshape(b, t, SSM_D_INNER)
    ssm_new = ssm_new.reshape(b, SSM_HEADS, P, N).astype(x.dtype)

    q = _partial_rope(_rmsnorm(q.reshape(b, t, 2 * ATT_HEADS, ATT_HEAD_DIM), p['q_norm_w']), q_pos)
    k = _partial_rope(_rmsnorm(k.reshape(b, t, 2 * ATT_HEADS, ATT_HEAD_DIM), p['k_norm_w']), q_pos)
    v = v.reshape(b, t, ATT_HEADS, 2 * ATT_HEAD_DIM)
    k_all = jnp.concatenate([k_past, k], axis=1)
    v_all = jnp.concatenate([v_past, v], axis=1)
    f32 = jnp.float32
    lam = (jnp.exp(jnp.sum(p['lambda_q1'].astype(f32) * p['lambda_k1'].astype(f32)))
           - jnp.exp(jnp.sum(p['lambda_q2'].astype(f32) * p['lambda_k2'].astype(f32))) + lambda_init)
    o = _diff_attention(q, k_all, v_all, q_pos, k_pos, lam)
    o = (_rmsnorm(o, p['subln_w']) * (1.0 - lambda_init)).reshape(b, t, ATT_V_WIDTH)

    mix = jax.nn.sigmoid(gate_ssm) * (y_ssm @ p['w_branch_ssm']) + jax.nn.sigmoid(gate_att) * (o @ p['w_branch_attn'])
    x = x + mix @ p['w_out']

    h = _rmsnorm(x, p['norm_ffn_w']) @ p['w_up']
    ha, hb = jnp.split(h, 2, axis=-1)
    hc, ffn_new = _causal_dwconv(ffn_past, ha, p['ffn_conv_w'], p['ffn_conv_b'])
    x = x + (jax.nn.silu(hc) * hb) @ p['w_down']
    return x, k, v, ssm_new, conv_new, ffn_new


def setup_inputs(seed: int = 0) -> dict:
    key = jax.random.key(seed)
    ks = list(jax.random.split(key, 32))

    def nrm(i, shape, scale):
        return scale * jax.random.normal(ks[i], shape, jnp.float32)

    L = DEPTH
    u_dt = jax.random.uniform(ks[9], (L, SSM_HEADS), jnp.float32)
    dt0 = jnp.exp(u_dt * (math.log(DT_MAX) - math.log(DT_MIN)) + math.log(DT_MIN))
    return {
        'x_prompt': nrm(0, (BATCH, SEQ, D_MODEL), 1.0),
        'x_sample': nrm(1, (DEC_BATCH, DEC_SEQ, D_MODEL), 1.0),
        'cache_k': nrm(2, (L, DEC_BATCH, PAST_LEN, 2 * ATT_HEADS, ATT_HEAD_DIM), 1.0),
        'cache_v': nrm(3, (L, DEC_BATCH, PAST_LEN, ATT_HEADS, 2 * ATT_HEAD_DIM), 1.0),
        'state_ssm': nrm(4, (L, DEC_BATCH, SSM_HEADS, SSM_HEAD_DIM, SSM_D_STATE), 0.5),
        'state_ssm_conv': nrm(5, (L, DEC_BATCH, SSM_CONV - 1, SSM_CONV_DIM), 1.0),
        'state_ffn_conv': nrm(6, (L, DEC_BATCH, FFN_CONV - 1, D_FF), 1.0),
        'norm_mix_w': 1.0 + nrm(7, (L, D_MODEL), 0.02),
        'w_in': nrm(8, (L, D_MODEL, IN_WIDTH), D_MODEL ** -0.5),
        'ssm_conv_w': nrm(10, (L, SSM_CONV, SSM_CONV_DIM), SSM_CONV ** -0.5),
        'ssm_conv_b': nrm(11, (L, SSM_CONV_DIM), 0.02),
        'ssm_dt_bias': dt0 + jnp.log(-jnp.expm1(-dt0)),
        'ssm_a_log': jnp.log(jax.random.uniform(ks[12], (L, SSM_HEADS), jnp.float32, 1.0, 16.0)),
        'ssm_d': 1.0 + nrm(13, (L, SSM_HEADS), 0.1),
        'ssm_norm_w': 1.0 + nrm(14, (L, SSM_D_INNER), 0.02),
        'q_norm_w': 1.0 + nrm(15, (L, ATT_HEAD_DIM), 0.02),
        'k_norm_w': 1.0 + nrm(16, (L, ATT_HEAD_DIM), 0.02),
        'lambda_q1': nrm(17, (L, ATT_HEAD_DIM), 0.1),
        'lambda_k1': nrm(18, (L, ATT_HEAD_DIM), 0.1),
        'lambda_q2': nrm(19, (L, ATT_HEAD_DIM), 0.1),
        'lambda_k2': nrm(20, (L, ATT_HEAD_DIM), 0.1),
        'subln_w': 1.0 + nrm(21, (L, 2 * ATT_HEAD_DIM), 0.02),
        'w_branch_ssm': nrm(22, (L, SSM_D_INNER, D_MODEL), SSM_D_INNER ** -0.5),
        'w_branch_attn': nrm(23, (L, ATT_V_WIDTH, D_MODEL), ATT_V_WIDTH ** -0.5),
        'w_out': nrm(24, (L, D_MODEL, D_MODEL), D_MODEL ** -0.5),
        'norm_ffn_w': 1.0 + nrm(25, (L, D_MODEL), 0.02),
        'w_up': nrm(26, (L, D_MODEL, 2 * D_FF), D_MODEL ** -0.5),
        'ffn_conv_w': nrm(27, (L, FFN_CONV, D_FF), FFN_CONV ** -0.5),
        'ffn_conv_b': nrm(28, (L, D_FF), 0.02),
        'w_down': nrm(29, (L, D_FF, D_MODEL), D_FF ** -0.5),
    }


def reference(x_prompt, x_sample, cache_k, cache_v, state_ssm, state_ssm_conv, state_ffn_conv,
              norm_mix_w, w_in, ssm_conv_w, ssm_conv_b, ssm_dt_bias, ssm_a_log, ssm_d, ssm_norm_w,
              q_norm_w, k_norm_w, lambda_q1, lambda_k1, lambda_q2, lambda_k2, subln_w,
              w_branch_ssm, w_branch_attn, w_out, norm_ffn_w, w_up, ffn_conv_w, ffn_conv_b, w_down):
    bp = x_prompt.shape[0]
    dt_ = x_prompt.dtype
    xp, xs = x_prompt, x_sample
    kp_l, vp_l, sp_l, cp_l, fp_l = [], [], [], [], []
    ks_l, vs_l, ss_l, cs_l, fs_l = [], [], [], [], []
    for layer in range(DEPTH):
        lambda_init = 0.8 - 0.6 * math.exp(-0.3 * layer)
        p = dict(norm_mix_w=norm_mix_w[layer], w_in=w_in[layer], ssm_conv_w=ssm_conv_w[layer],
                 ssm_conv_b=ssm_conv_b[layer], ssm_dt_bias=ssm_dt_bias[layer], ssm_a_log=ssm_a_log[layer],
                 ssm_d=ssm_d[layer], ssm_norm_w=ssm_norm_w[layer], q_norm_w=q_norm_w[layer],
                 k_norm_w=k_norm_w[layer], lambda_q1=lambda_q1[layer], lambda_k1=lambda_k1[layer],
                 lambda_q2=lambda_q2[layer], lambda_k2=lambda_k2[layer], subln_w=subln_w[layer],
                 w_branch_ssm=w_branch_ssm[layer], w_branch_attn=w_branch_attn[layer], w_out=w_out[layer],
                 norm_ffn_w=norm_ffn_w[layer], w_up=w_up[layer], ffn_conv_w=ffn_conv_w[layer],
                 ffn_conv_b=ffn_conv_b[layer], w_down=w_down[layer])
        xp, kp, vp, sp, cp, fp = _layer(
            xp,
            jnp.zeros((bp, 0, 2 * ATT_HEADS, ATT_HEAD_DIM), dt_),
            jnp.zeros((bp, 0, ATT_HEADS, 2 * ATT_HEAD_DIM), dt_),
            jnp.zeros((bp, SSM_HEADS, SSM_HEAD_DIM, SSM_D_STATE), dt_),
            jnp.zeros((bp, SSM_CONV - 1, SSM_CONV_DIM), dt_),
            jnp.zeros((bp, FFN_CONV - 1, D_FF), dt_),
            lambda_init, p)
        xs, ksn, vsn, ssn, csn, fsn = _layer(
            xs, cache_k[layer], cache_v[layer], state_ssm[layer], state_ssm_conv[layer],
            state_ffn_conv[layer], lambda_init, p)
        kp_l.append(kp); vp_l.append(vp); sp_l.append(sp); cp_l.append(cp); fp_l.append(fp)
        ks_l.append(ksn); vs_l.append(vsn); ss_l.append(ssn); cs_l.append(csn); fs_l.append(fsn)
    return (xp, xs,
            jnp.stack(kp_l), jnp.stack(vp_l), jnp.stack(sp_l), jnp.stack(cp_l), jnp.stack(fp_l),
            jnp.stack(ks_l), jnp.stack(vs_l), jnp.stack(ss_l), jnp.stack(cs_l), jnp.stack(fs_l))
```

```python
import functools
import math

import jax
import jax.numpy as jnp
from jax import lax
from jax.experimental import pallas as pl
from jax.experimental.pallas import tpu as pltpu

F32 = jnp.float32
BF16 = jnp.bfloat16

D_MODEL = 1024
CHUNK = 64
NORM_EPS = 1e-6

SSM_D_INNER = 2048
SSM_HEAD_DIM = 64
SSM_HEADS = 32
SSM_GROUPS = 4
SSM_D_STATE = 128
SSM_CONV = 4
SSM_CONV_DIM = 3072
SSM_GROUP_WIDTH = SSM_D_INNER // SSM_GROUPS
SSM_PAIRS_PER_GROUP = SSM_GROUP_WIDTH // 128

ATT_HEADS = 8
ATT_HEAD_DIM = 64
ATT_WIDTH = 1024
ROPE_THETA = 500000.0
ROT_DIM = 16

D_FF = 2816
FFN_CONV = 3

IN_SIZES = (SSM_D_INNER, SSM_CONV_DIM, SSM_HEADS, ATT_WIDTH, ATT_WIDTH, ATT_WIDTH, D_MODEL, D_MODEL)

PROJ_WIDTH = 10240
COL_Z, COL_Q, COL_K, COL_V, COL_GS, COL_XBC, COL_GA = 0, 2048, 3072, 4096, 5120, 6144, 9216

LANES = 128
NEG_BIG = -1e30
VMEM_LIMIT_MB = 56


def _cparams(semantics):
    return pltpu.CompilerParams(dimension_semantics=semantics,
                                vmem_limit_bytes=VMEM_LIMIT_MB * 1024 * 1024)


def _sigmoid(x):
    return 1.0 / (1.0 + jnp.exp(-x))


def _silu(x):
    return x * _sigmoid(x)


def _softplus(x):
    return jnp.maximum(x, 0.0) + jnp.log1p(jnp.exp(-jnp.abs(x)))


def _inproj_kernel(x_ref, nw_ref, w_ref, wdt_ref, proj_ref, dt_ref, xn_ref):
    @pl.when(pl.program_id(1) == 0)
    def _():
        x = x_ref[...]
        xn = x * lax.rsqrt(jnp.mean(x * x, axis=-1, keepdims=True) + NORM_EPS) * nw_ref[...]
        xn_ref[...] = xn.astype(BF16)
        dt_ref[...] = jnp.dot(xn_ref[...], wdt_ref[...], preferred_element_type=F32)

    proj_ref[...] = jnp.dot(xn_ref[...], w_ref[...], preferred_element_type=F32)


def _in_proj(x2d, norm_w, w_main, w_dt):
    n = x2d.shape[0]
    tm = min(n, 1024)
    tn = 2048
    return pl.pallas_call(
        _inproj_kernel,
        grid=(n // tm, PROJ_WIDTH // tn),
        in_specs=[
            pl.BlockSpec((tm, D_MODEL), lambda i, j: (i, 0)),
            pl.BlockSpec((1, D_MODEL), lambda i, j: (0, 0)),
            pl.BlockSpec((D_MODEL, tn), lambda i, j: (0, j)),
            pl.BlockSpec((D_MODEL, LANES), lambda i, j: (0, 0)),
        ],
        out_specs=[
            pl.BlockSpec((tm, tn), lambda i, j: (i, j)),
            pl.BlockSpec((tm, LANES), lambda i, j: (i, 0)),
        ],
        out_shape=[
            jax.ShapeDtypeStruct((n, PROJ_WIDTH), F32),
            jax.ShapeDtypeStruct((n, LANES), F32),
        ],
        scratch_shapes=[pltpu.VMEM((tm, D_MODEL), BF16)],
        compiler_params=_cparams(("parallel", "arbitrary")),
        name="in_proj",
    )(x2d, norm_w, w_main, w_dt)


def _qkprep_kernel(q_ref, k_ref, v_ref, cos_ref, sn_ref, sp_ref, qw_ref, kw_ref, bd_ref,
                   qb_ref, kf_ref, kb_ref, vf_ref, vb_ref, *, q_scale):
    bd = bd_ref[...]
    cos, sn, sp = cos_ref[...], sn_ref[...], sp_ref[...]

    def norm_rope(x, w):
        x2 = x * x
        hi = x2.astype(BF16)
        lo = (x2 - hi.astype(F32)).astype(BF16)
        ss = (jnp.dot(hi, bd, preferred_element_type=F32)
              + jnp.dot(lo, bd, preferred_element_type=F32))
        y = x * lax.rsqrt(ss * (1.0 / ATT_HEAD_DIM) + NORM_EPS) * w
        return (y * cos + pltpu.roll(y, LANES - ROT_DIM // 2, 1) * sn
                + pltpu.roll(y, ROT_DIM // 2, 1) * sp)

    for g in range(ATT_WIDTH // LANES):
        sl = slice(g * LANES, (g + 1) * LANES)
        qr = norm_rope(q_ref[:, sl], qw_ref[...])
        qb_ref[:, sl] = (qr * q_scale).astype(BF16)
        kr = norm_rope(k_ref[:, sl], kw_ref[...])
        kf_ref[:, sl] = kr
        kb_ref[:, sl] = kr.astype(BF16)
    v = v_ref[...]
    vf_ref[...] = v
    vb_ref[...] = v.astype(BF16)


def _qk_prep(proj, tables, qw, kw, bd, t):
    n = proj.shape[0]
    tm = min(t, 512)
    nt = t // tm
    cos, sn, sp = tables
    wide = lambda c: pl.BlockSpec((tm, ATT_WIDTH), lambda i: (i, c // ATT_WIDTH))
    tab = pl.BlockSpec((tm, LANES), lambda i: (i % nt, 0))
    vec = pl.BlockSpec((1, LANES), lambda i: (0, 0))
    out = pl.BlockSpec((tm, ATT_WIDTH), lambda i: (i, 0))
    q_scale = ATT_HEAD_DIM ** -0.5 * math.log2(math.e)
    return pl.pallas_call(
        functools.partial(_qkprep_kernel, q_scale=q_scale),
        grid=(n // tm,),
        in_specs=[wide(COL_Q), wide(COL_K), wide(COL_V), tab, tab, tab, vec, vec,
                  pl.BlockSpec((LANES, LANES), lambda i: (0, 0))],
        out_specs=[out, out, out, out, out],
        out_shape=[
            jax.ShapeDtypeStruct((n, ATT_WIDTH), BF16),
            jax.ShapeDtypeStruct((n, ATT_WIDTH), F32),
            jax.ShapeDtypeStruct((n, ATT_WIDTH), BF16),
            jax.ShapeDtypeStruct((n, ATT_WIDTH), F32),
            jax.ShapeDtypeStruct((n, ATT_WIDTH), BF16),
        ],
        compiler_params=_cparams(("parallel",)),
        name="qk_prep",
    )(proj, proj, proj, cos, sn, sp, qw, kw, bd)


def _rope_tables(t, past_len):
    half = ROT_DIM // 2
    inv_freq = jnp.power(ROPE_THETA, -jnp.arange(half, dtype=F32) * 2.0 / ROT_DIM)
    pos = past_len + jnp.arange(t, dtype=jnp.int32)
    ang = pos.astype(F32)[:, None] * inv_freq[None, :]
    cos, sin = jnp.cos(ang), jnp.sin(ang)
    ones = jnp.ones((t, ATT_HEAD_DIM - ROT_DIM), F32)
    zeros_h = jnp.zeros((t, half), F32)
    zeros_r = jnp.zeros((t, ATT_HEAD_DIM - ROT_DIM), F32)
    cos_h = jnp.concatenate([cos, cos, ones], axis=1)
    sn_h = jnp.concatenate([-sin, zeros_h, zeros_r], axis=1)
    sp_h = jnp.concatenate([zeros_h, sin, zeros_r], axis=1)
    two = lambda a: jnp.concatenate([a, a], axis=1)
    return two(cos_h), two(sn_h), two(sp_h)


def _lam_kernel(q1_ref, k1_ref, q2_ref, k2_ref, out_ref, *, lambda_init):
    a = jnp.sum(q1_ref[...] * k1_ref[...], axis=-1, keepdims=True)
    b = jnp.sum(q2_ref[...] * k2_ref[...], axis=-1, keepdims=True)
    out_ref[...] = jnp.exp(a) - jnp.exp(b) + lambda_init


def _lambda(q1, k1, q2, k2, lambda_init):
    return pl.pallas_call(
        functools.partial(_lam_kernel, lambda_init=lambda_init),
        out_shape=jax.ShapeDtypeStruct((1, 1), F32),
        name="lambda",
    )(q1, k1, q2, k2)


def _attn_kernel(lam_ref, q_ref, k_ref, v_ref, sw_ref, o_ref, qm_ref, m_ref, l_ref, acc_ref,
                 *, tq, tk, past_len, out_scale):
    i = pl.program_id(1)
    j = pl.program_id(2)
    q0 = past_len + i * tq
    j_last = (((q0 + tq - 1) // CHUNK) * CHUNK + CHUNK - 1) // tk
    needs_mask = (j * tk + tk - 1) // CHUNK > q0 // CHUNK

    @pl.when(j == 0)
    def _():
        lo_half = lax.broadcasted_iota(jnp.int32, (tq, LANES), 1) < ATT_HEAD_DIM
        for h in range(ATT_HEADS):
            qh = q_ref[0, :, h * LANES:(h + 1) * LANES]
            zero = jnp.zeros_like(qh)
            qm_ref[h, 0:tq, :] = jnp.where(lo_half, qh, zero)
            qm_ref[h, tq:2 * tq, :] = jnp.where(lo_half, zero, qh)
        m_ref[...] = jnp.full(m_ref.shape, NEG_BIG, F32)
        l_ref[...] = jnp.zeros(l_ref.shape, F32)
        acc_ref[...] = jnp.zeros(acc_ref.shape, F32)

    def step(masked):
        if masked:
            row = lax.broadcasted_iota(jnp.int32, (2 * tq, tk), 0)
            col = lax.broadcasted_iota(jnp.int32, (2 * tq, tk), 1)
            visible = ((j * tk + col) >> 6) <= ((q0 + (row & (tq - 1))) >> 6)
        for h in range(ATT_HEADS):
            hs = slice(h * LANES, (h + 1) * LANES)
            s = lax.dot_general(qm_ref[h], k_ref[0, :, hs], (((1,), (1,)), ((), ())),
                                preferred_element_type=F32)
            if masked:
                s = jnp.where(visible, s, NEG_BIG)
            m_prev = m_ref[h]
            m_new = jnp.maximum(m_prev, jnp.max(s, axis=-1, keepdims=True))
            alpha = jnp.exp2(m_prev - m_new)
            p = jnp.exp2(s - jnp.concatenate([m_new] * (tk // LANES), axis=1))
            l_ref[h] = alpha * l_ref[h] + jnp.sum(p, axis=-1, keepdims=True)
            acc_ref[h] = acc_ref[h] * alpha + jnp.dot(p.astype(BF16), v_ref[0, :, hs],
                                                      preferred_element_type=F32)
            m_ref[h] = m_new

    @pl.when(jnp.logical_and(j <= j_last, needs_mask))
    def _():
        step(True)

    @pl.when(jnp.logical_and(j <= j_last, jnp.logical_not(needs_mask)))
    def _():
        step(False)

    @pl.when(j == j_last)
    def _():
        lam = lam_ref[0, 0]
        for h in range(ATT_HEADS):
            hs = slice(h * LANES, (h + 1) * LANES)
            o = acc_ref[h] / l_ref[h]
            o = o[0:tq] - lam * o[tq:2 * tq]
            on = o * lax.rsqrt(jnp.mean(o * o, axis=-1, keepdims=True) + NORM_EPS) * sw_ref[...]
            o_ref[0, :, hs] = (on * out_scale).astype(BF16)


def _key_block(s_len):
    return min(512, -(-s_len // LANES) * LANES)


def _attention(lam, q, k, v, subln_w, past_len, out_scale):
    b, t, _ = q.shape
    s_len = k.shape[1]
    tq = min(t, 256)
    tk = _key_block(s_len)
    assert tq & (tq - 1) == 0 and CHUNK == 64 and s_len % tk == 0 and t % tq == 0 and tk % LANES == 0
    nq, nk = t // tq, s_len // tk

    def kv_map(bi, i, j):
        j_last = (((past_len + i * tq + tq - 1) // CHUNK) * CHUNK + CHUNK - 1) // tk
        return (bi, jnp.minimum(j, j_last), 0)

    return pl.pallas_call(
        functools.partial(_attn_kernel, tq=tq, tk=tk, past_len=past_len, out_scale=out_scale),
        grid=(b, nq, nk),
        in_specs=[
            pl.BlockSpec(memory_space=pltpu.SMEM),
            pl.BlockSpec((1, tq, ATT_WIDTH), lambda bi, i, j: (bi, i, 0)),
            pl.BlockSpec((1, tk, ATT_WIDTH), kv_map),
            pl.BlockSpec((1, tk, ATT_WIDTH), kv_map),
            pl.BlockSpec((1, LANES), lambda bi, i, j: (0, 0)),
        ],
        out_specs=pl.BlockSpec((1, tq, ATT_WIDTH), lambda bi, i, j: (bi, i, 0)),
        out_shape=jax.ShapeDtypeStruct((b, t, ATT_WIDTH), BF16),
        scratch_shapes=[
            pltpu.VMEM((ATT_HEADS, 2 * tq, LANES), BF16),
            pltpu.VMEM((ATT_HEADS, 2 * tq, LANES), F32),
            pltpu.VMEM((ATT_HEADS, 2 * tq, LANES), F32),
            pltpu.VMEM((ATT_HEADS, 2 * tq, LANES), F32),
        ],
        compiler_params=_cparams(("parallel", "parallel", "arbitrary")),
        name="attention",
    )(lam, q, k, v, subln_w)


def _ssd_kernel(xbc_ref, z_ref, dt_ref, cw_ref, cb_ref, dtb_ref, alog_ref, dexp_ref, nw_ref, tril_ref,
                cpast_ref, spast_ref, y_ref, st_ref, xpad_ref, xcv_ref, state_ref, *, tt):
    ti = pl.program_id(1)
    L = CHUNK
    hist = SSM_CONV - 1
    base = 8

    @pl.when(ti == 0)
    def _():
        xpad_ref[base - hist:base, :] = cpast_ref[0]
        state_ref[...] = spast_ref[0]

    xpad_ref[base:base + tt, :] = xbc_ref[0]
    for c in range(tt // L):
        r = base + c * L
        acc = cb_ref[...] + cw_ref[hist:hist + 1, :] * xpad_ref[r:r + L, :]
        for d in range(1, SSM_CONV):
            acc = acc + cw_ref[hist - d:hist - d + 1, :] * xpad_ref[r - d:r - d + L, :]
        xcv_ref[c * L:(c + 1) * L, :] = _silu(acc)
    xpad_ref[base - hist:base, :] = xpad_ref[base + tt - hist:base + tt, :]

    lane = lax.broadcasted_iota(jnp.int32, (L, LANES), 1)
    lo_half = lane < SSM_HEAD_DIM
    tril2 = (lane & (SSM_HEAD_DIM - 1)) <= lax.broadcasted_iota(jnp.int32, (L, LANES), 0)
    a_neg = -jnp.exp(alog_ref[...])
    tril = tril_ref[...]
    b_off = SSM_D_INNER
    c_off = SSM_D_INNER + SSM_GROUPS * SSM_D_STATE

    def chunk(c, carry):
        r0 = pl.multiple_of(c * L, L)
        rows = pl.ds(r0, L)
        dtv = _softplus(dt_ref[0, rows, :] + dtb_ref[...])
        da = dtv * a_neg
        hi = da.astype(BF16)
        r1 = da - hi.astype(F32)
        mid = r1.astype(BF16)
        lo = (r1 - mid.astype(F32)).astype(BF16)
        acum = (jnp.dot(tril, hi, preferred_element_type=F32)
                + jnp.dot(tril, mid, preferred_element_type=F32)
                + jnp.dot(tril, lo, preferred_element_type=F32))
        acum_t = jnp.concatenate([acum, acum], axis=0).T
        for g in range(SSM_GROUPS):
            gs = slice(g * SSM_GROUP_WIDTH, (g + 1) * SSM_GROUP_WIDTH)
            bm = xcv_ref[rows, b_off + g * SSM_D_STATE:b_off + (g + 1) * SSM_D_STATE].astype(BF16)
            cm = xcv_ref[rows, c_off + g * SSM_D_STATE:c_off + (g + 1) * SSM_D_STATE].astype(BF16)
            cb2 = lax.dot_general(cm, jnp.concatenate([bm, bm], axis=0), (((1,), (1,)), ((), ())),
                                  preferred_element_type=F32)
            st = state_ref[g]
            y_in = jnp.dot(cm, st.astype(BF16), preferred_element_type=F32)
            ys, xws, lasts = [], [], []
            for pp in range(SSM_PAIRS_PER_GROUP):
                pair = g * SSM_PAIRS_PER_GROUP + pp
                h1, h2 = 2 * pair, 2 * pair + 1
                ps = slice(pair * LANES, (pair + 1) * LANES)
                a_col = jnp.where(lo_half, acum[:, h1:h1 + 1], acum[:, h2:h2 + 1])
                dt_col = jnp.where(lo_half, dtv[:, h1:h1 + 1], dtv[:, h2:h2 + 1])
                a_row = jnp.where(lo_half[0:1], acum_t[h1:h1 + 1, :], acum_t[h2:h2 + 1, :])
                last = a_col[L - 1:L, :]
                xs = xcv_ref[rows, ps]
                xdt = xs * dt_col
                w = (cb2 * jnp.exp(jnp.where(tril2, a_col - a_row, NEG_BIG))).astype(BF16)
                zero = jnp.zeros_like(xdt)
                rhs = jnp.concatenate([jnp.where(lo_half, xdt, zero), jnp.where(lo_half, zero, xdt)],
                                      axis=0).astype(BF16)
                y = (jnp.dot(w, rhs, preferred_element_type=F32)
                     + y_in[:, pp * LANES:(pp + 1) * LANES] * jnp.exp(a_col)
                     + dexp_ref[:, ps] * xs)
                ys.append(y)
                xws.append(xdt * jnp.exp(last - a_col))
                lasts.append(last)
            yg = jnp.concatenate(ys, axis=1)
            xwg = jnp.concatenate(xws, axis=1).astype(BF16)
            lastg = jnp.concatenate(lasts, axis=1)
            state_ref[g] = st * jnp.exp(lastg) + lax.dot_general(
                bm, xwg, (((0,), (0,)), ((), ())), preferred_element_type=F32)
            yg = yg * _silu(z_ref[0, rows, gs])
            yn = yg * lax.rsqrt(jnp.mean(yg * yg, axis=-1, keepdims=True) + NORM_EPS) * nw_ref[:, gs]
            y_ref[0, rows, gs] = yn.astype(BF16)
        return carry

    lax.fori_loop(0, tt // L, chunk, 0)

    @pl.when(ti == pl.num_programs(1) - 1)
    def _():
        st_ref[0] = state_ref[...]


def _ssd(proj3, dt3, conv_past, state_past_t, cw, cb, dtb, alog, dexp, nw, tril):
    b, t, _ = proj3.shape
    tt = min(t, 256)
    full = lambda shape: pl.BlockSpec(shape, lambda bi, i: (0,) * len(shape))
    return pl.pallas_call(
        functools.partial(_ssd_kernel, tt=tt),
        grid=(b, t // tt),
        in_specs=[
            pl.BlockSpec((1, tt, SSM_CONV_DIM), lambda bi, i: (bi, i, COL_XBC // SSM_CONV_DIM)),
            pl.BlockSpec((1, tt, SSM_D_INNER), lambda bi, i: (bi, i, COL_Z // SSM_D_INNER)),
            pl.BlockSpec((1, tt, LANES), lambda bi, i: (bi, i, 0)),
            full((SSM_CONV, SSM_CONV_DIM)), full((1, SSM_CONV_DIM)), full((1, LANES)), full((1, LANES)),
            full((1, SSM_D_INNER)), full((1, SSM_D_INNER)), full((CHUNK, CHUNK)),
            pl.BlockSpec((1, SSM_CONV - 1, SSM_CONV_DIM), lambda bi, i: (bi, 0, 0)),
            pl.BlockSpec((1, SSM_GROUPS, SSM_D_STATE, SSM_GROUP_WIDTH), lambda bi, i: (bi, 0, 0, 0)),
        ],
        out_specs=[
            pl.BlockSpec((1, tt, SSM_D_INNER), lambda bi, i: (bi, i, 0)),
            pl.BlockSpec((1, SSM_GROUPS, SSM_D_STATE, SSM_GROUP_WIDTH), lambda bi, i: (bi, 0, 0, 0)),
        ],
        out_shape=[
            jax.ShapeDtypeStruct((b, t, SSM_D_INNER), BF16),
            jax.ShapeDtypeStruct((b, SSM_GROUPS, SSM_D_STATE, SSM_GROUP_WIDTH), F32),
        ],
        scratch_shapes=[
            pltpu.VMEM((tt + 8, SSM_CONV_DIM), F32),
            pltpu.VMEM((tt, SSM_CONV_DIM), F32),
            pltpu.VMEM((SSM_GROUPS, SSM_D_STATE, SSM_GROUP_WIDTH), F32),
        ],
        compiler_params=_cparams(("parallel", "arbitrary")),
        name="ssd",
    )(proj3, proj3, dt3, cw, cb, dtb, alog, dexp, nw, tril, conv_past, state_past_t)


def _merge_kernel(x_ref, ys_ref, o_ref, gs_ref, ga_ref, wbs_ref, wba_ref, wo_ref, nfw_ref, x1_ref, hn_ref):
    bs = jnp.dot(ys_ref[...], wbs_ref[...], preferred_element_type=F32)
    ba = jnp.dot(o_ref[...], wba_ref[...], preferred_element_type=F32)
    mix = _sigmoid(gs_ref[...]) * bs + _sigmoid(ga_ref[...]) * ba
    x1 = x_ref[...] + jnp.dot(mix.astype(BF16), wo_ref[...], preferred_element_type=F32)
    x1_ref[...] = x1
    hn = x1 * lax.rsqrt(jnp.mean(x1 * x1, axis=-1, keepdims=True) + NORM_EPS) * nfw_ref[...]
    hn_ref[...] = hn.astype(BF16)


def _merge(x2d, ys, o, proj, wbs, wba, wo, nfw):
    n = x2d.shape[0]
    tm = min(n, 512)
    row = lambda w, c=0: pl.BlockSpec((tm, w), lambda i: (i, c))
    full = lambda shape: pl.BlockSpec(shape, lambda i: (0, 0))
    return pl.pallas_call(
        _merge_kernel,
        grid=(n // tm,),
        in_specs=[row(D_MODEL), row(SSM_D_INNER), row(ATT_WIDTH),
                  row(D_MODEL, COL_GS // D_MODEL), row(D_MODEL, COL_GA // D_MODEL),
                  full((SSM_D_INNER, D_MODEL)), full((ATT_WIDTH, D_MODEL)), full((D_MODEL, D_MODEL)),
                  full((1, D_MODEL))],
        out_specs=[row(D_MODEL), row(D_MODEL)],
        out_shape=[jax.ShapeDtypeStruct((n, D_MODEL), F32), jax.ShapeDtypeStruct((n, D_MODEL), BF16)],
        compiler_params=_cparams(("parallel",)),
        name="merge",
    )(x2d, ys, o, proj, proj, wbs, wba, wo, nfw)


FFN_SPLIT = 2


def _ffn_kernel(hn_ref, x1_ref, wa_ref, wb_ref, wd_ref, cw_ref, cb_ref, past_ref, out_ref, tail_ref,
                hbuf_ref, *, tm):
    ti = pl.program_id(1)
    hist = FFN_CONV - 1
    base = 8
    tf = D_FF // FFN_SPLIT

    @pl.when(ti == 0)
    def _():
        hbuf_ref[base - hist:base, :] = past_ref[0]

    hn = hn_ref[0]
    acc = x1_ref[0]
    for f in range(FFN_SPLIT):
        fs = slice(f * tf, (f + 1) * tf)
        ha = jnp.dot(hn, wa_ref[:, fs], preferred_element_type=F32)
        hb = jnp.dot(hn, wb_ref[:, fs], preferred_element_type=F32)
        hbuf_ref[base:base + tm, fs] = ha
        hc = cb_ref[:, fs] + cw_ref[hist:hist + 1, fs] * ha
        for d in range(1, FFN_CONV):
            hc = hc + cw_ref[hist - d:hist - d + 1, fs] * hbuf_ref[base - d:base - d + tm, fs]
        u = (_silu(hc) * hb).astype(BF16)
        acc = acc + jnp.dot(u, wd_ref[fs, :], preferred_element_type=F32)
    out_ref[0] = acc
    tail = hbuf_ref[base + tm - hist:base + tm, :]
    hbuf_ref[base - hist:base, :] = tail
    tail_ref[0] = tail


def _ffn(hn3, x13, wa, wb, wd, cw, cb, past):
    b, t, _ = hn3.shape
    tm = min(t, 512)
    const = lambda shape: pl.BlockSpec(shape, lambda bi, i: (0, 0), pipeline_mode=pl.Buffered(1))
    return pl.pallas_call(
        functools.partial(_ffn_kernel, tm=tm),
        grid=(b, t // tm),
        in_specs=[
            pl.BlockSpec((1, tm, D_MODEL), lambda bi, i: (bi, i, 0)),
            pl.BlockSpec((1, tm, D_MODEL), lambda bi, i: (bi, i, 0)),
            const((D_MODEL, D_FF)), const((D_MODEL, D_FF)), const((D_FF, D_MODEL)),
            const((FFN_CONV, D_FF)), const((1, D_FF)),
            pl.BlockSpec((1, FFN_CONV - 1, D_FF), lambda bi, i: (bi, 0, 0)),
        ],
        out_specs=[
            pl.BlockSpec((1, tm, D_MODEL), lambda bi, i: (bi, i, 0)),
            pl.BlockSpec((1, FFN_CONV - 1, D_FF), lambda bi, i: (bi, 0, 0)),
        ],
        out_shape=[
            jax.ShapeDtypeStruct((b, t, D_MODEL), F32),
            jax.ShapeDtypeStruct((b, FFN_CONV - 1, D_FF), F32),
        ],
        scratch_shapes=[pltpu.VMEM((tm + 8, D_FF), F32)],
        compiler_params=_cparams(("parallel", "arbitrary")),
        name="ffn",
    )(hn3, x13, wa, wb, wd, cw, cb, past)


def _prepare_weights(norm_mix_w, w_in, ssm_conv_w, ssm_conv_b, ssm_dt_bias, ssm_a_log, ssm_d, ssm_norm_w,
                     q_norm_w, k_norm_w, subln_w, w_branch_ssm, w_branch_attn, w_out, norm_ffn_w, w_up,
                     ffn_conv_w, ffn_conv_b, w_down):
    bounds = [0]
    for s in IN_SIZES:
        bounds.append(bounds[-1] + s)
    piece = lambda n: w_in[:, bounds[n]:bounds[n + 1]]
    z, xbc, dt, q, k, v, gs, ga = (piece(n) for n in range(8))
    pad_heads = lambda a: jnp.pad(a.reshape(1, SSM_HEADS), ((0, 0), (0, LANES - SSM_HEADS)))
    row = lambda a: a.reshape(1, -1)
    two_heads = lambda a: jnp.concatenate([a, a]).reshape(1, LANES)
    head = jnp.arange(LANES) // ATT_HEAD_DIM
    tri = jnp.arange(CHUNK)
    return dict(
        norm_mix=row(norm_mix_w),
        w_main=jnp.concatenate([z, q, k, v, gs, xbc, ga], axis=1).astype(BF16),
        w_dt=jnp.pad(dt, ((0, 0), (0, LANES - SSM_HEADS))).astype(BF16),
        conv_w=ssm_conv_w, conv_b=row(ssm_conv_b),
        dt_bias=pad_heads(ssm_dt_bias), a_log=pad_heads(ssm_a_log),
        d_exp=row(jnp.repeat(ssm_d, SSM_HEAD_DIM)), ssm_norm=row(ssm_norm_w),
        tril=(tri[:, None] >= tri[None, :]).astype(BF16),
        q_norm=two_heads(q_norm_w), k_norm=two_heads(k_norm_w),
        head_ones=(head[:, None] == head[None, :]).astype(BF16),
        subln=row(subln_w),
        w_bs=w_branch_ssm.astype(BF16), w_ba=w_branch_attn.astype(BF16), w_out=w_out.astype(BF16),
        norm_ffn=row(norm_ffn_w),
        w_up_a=w_up[:, :D_FF].astype(BF16), w_up_b=w_up[:, D_FF:].astype(BF16),
        w_down=w_down.astype(BF16), ffn_conv_w=ffn_conv_w, ffn_conv_b=row(ffn_conv_b),
    )


def _state_to_kernel_layout(s):
    b = s.shape[0]
    s = s.reshape(b, SSM_GROUPS, SSM_HEADS // SSM_GROUPS, SSM_HEAD_DIM, SSM_D_STATE)
    return jnp.transpose(s, (0, 1, 4, 2, 3)).reshape(b, SSM_GROUPS, SSM_D_STATE, SSM_GROUP_WIDTH)


def _state_from_kernel_layout(s):
    b = s.shape[0]
    s = s.reshape(b, SSM_GROUPS, SSM_D_STATE, SSM_HEADS // SSM_GROUPS, SSM_HEAD_DIM)
    return jnp.transpose(s, (0, 1, 3, 4, 2)).reshape(b, SSM_HEADS, SSM_HEAD_DIM, SSM_D_STATE)


def _layer(x, k_past, v_past, ssm_past, conv_past, ffn_past, lam, lambda_init, w):
    b, t, _ = x.shape
    past_len = k_past.shape[1]
    n = b * t
    x2d = x.reshape(n, D_MODEL)

    proj, dt = _in_proj(x2d, w["norm_mix"], w["w_main"], w["w_dt"])
    qb, kf, kb, vf, vb = _qk_prep(proj, _rope_tables(t, past_len), w["q_norm"], w["k_norm"],
                                  w["head_ones"], t)

    k_all = kb.reshape(b, t, ATT_WIDTH)
    v_all = vb.reshape(b, t, ATT_WIDTH)
    if past_len:
        k_all = jnp.concatenate([k_past.reshape(b, past_len, ATT_WIDTH).astype(BF16), k_all], axis=1)
        v_all = jnp.concatenate([v_past.reshape(b, past_len, ATT_WIDTH).astype(BF16), v_all], axis=1)
    s_len = past_len + t
    pad = (-s_len) % _key_block(s_len)
    if pad:
        k_all = jnp.pad(k_all, ((0, 0), (0, pad), (0, 0)))
        v_all = jnp.pad(v_all, ((0, 0), (0, pad), (0, 0)))
    o = _attention(lam, qb.reshape(b, t, ATT_WIDTH), k_all, v_all, w["subln"], past_len,
                   1.0 - lambda_init)

    ys, state_t = _ssd(proj.reshape(b, t, PROJ_WIDTH), dt.reshape(b, t, LANES), conv_past,
                       _state_to_kernel_layout(ssm_past), w["conv_w"], w["conv_b"], w["dt_bias"],
                       w["a_log"], w["d_exp"], w["ssm_norm"], w["tril"])

    x1, hn = _merge(x2d, ys.reshape(n, SSM_D_INNER), o.reshape(n, ATT_WIDTH), proj,
                    w["w_bs"], w["w_ba"], w["w_out"], w["norm_ffn"])
    y, ffn_new = _ffn(hn.reshape(b, t, D_MODEL), x1.reshape(b, t, D_MODEL), w["w_up_a"], w["w_up_b"],
                      w["w_down"], w["ffn_conv_w"], w["ffn_conv_b"], ffn_past)

    xbc_rows = proj.reshape(b, t, PROJ_WIDTH)[:, t - (SSM_CONV - 1):, COL_XBC:COL_XBC + SSM_CONV_DIM]
    k_new = kf.reshape(b, t, 2 * ATT_HEADS, ATT_HEAD_DIM)
    v_new = vf.reshape(b, t, ATT_HEADS, 2 * ATT_HEAD_DIM)
    return y, k_new, v_new, _state_from_kernel_layout(state_t), xbc_rows, ffn_new


def kernel(x_prompt, x_sample, cache_k, cache_v, state_ssm, state_ssm_conv, state_ffn_conv, norm_mix_w, w_in, ssm_conv_w, ssm_conv_b, ssm_dt_bias, ssm_a_log, ssm_d, ssm_norm_w, q_norm_w, k_norm_w, lambda_q1, lambda_k1, lambda_q2, lambda_k2, subln_w, w_branch_ssm, w_branch_attn, w_out, norm_ffn_w, w_up, ffn_conv_w, ffn_conv_b, w_down):
    depth = w_in.shape[0]
    assert depth == 1
    bp = x_prompt.shape[0]
    dt_ = x_prompt.dtype
    layer = 0
    lambda_init = 0.8 - 0.6 * math.exp(-0.3 * layer)
    w = _prepare_weights(norm_mix_w[layer], w_in[layer], ssm_conv_w[layer], ssm_conv_b[layer],
                         ssm_dt_bias[layer], ssm_a_log[layer], ssm_d[layer], ssm_norm_w[layer],
                         q_norm_w[layer], k_norm_w[layer], subln_w[layer], w_branch_ssm[layer],
                         w_branch_attn[layer], w_out[layer], norm_ffn_w[layer], w_up[layer],
                         ffn_conv_w[layer], ffn_conv_b[layer], w_down[layer])
    lam = _lambda(lambda_q1[layer].reshape(1, -1), lambda_k1[layer].reshape(1, -1),
                  lambda_q2[layer].reshape(1, -1), lambda_k2[layer].reshape(1, -1), lambda_init)

    yp, kp, vp, sp, cp, fp = _layer(
        x_prompt,
        jnp.zeros((bp, 0, 2 * ATT_HEADS, ATT_HEAD_DIM), dt_),
        jnp.zeros((bp, 0, ATT_HEADS, 2 * ATT_HEAD_DIM), dt_),
        jnp.zeros((bp, SSM_HEADS, SSM_HEAD_DIM, SSM_D_STATE), dt_),
        jnp.zeros((bp, SSM_CONV - 1, SSM_CONV_DIM), dt_),
        jnp.zeros((bp, FFN_CONV - 1, D_FF), dt_),
        lam, lambda_init, w)
    ys, ks, vs, ss, cs, fs = _layer(
        x_sample, cache_k[layer], cache_v[layer], state_ssm[layer], state_ssm_conv[layer],
        state_ffn_conv[layer], lam, lambda_init, w)
    stack = lambda a: a[None]
    return (yp, ys, stack(kp), stack(vp), stack(sp), stack(cp), stack(fp),
            stack(ks), stack(vs), stack(ss), stack(cs), stack(fs))
```

```python
import functools
import math

import jax
import jax.numpy as jnp
from jax import lax
from jax.experimental import pallas as pl
from jax.experimental.pallas import tpu as pltpu

F32 = jnp.float32
BF16 = jnp.bfloat16

D_MODEL = 1024
CHUNK = 64
NORM_EPS = 1e-6

SSM_D_INNER = 2048
SSM_HEAD_DIM = 64
SSM_HEADS = 32
SSM_GROUPS = 4
SSM_D_STATE = 128
SSM_CONV = 4
SSM_CONV_DIM = 3072
SSM_GROUP_WIDTH = SSM_D_INNER // SSM_GROUPS
SSM_PAIRS_PER_GROUP = SSM_GROUP_WIDTH // 128

ATT_HEADS = 8
ATT_HEAD_DIM = 64
ATT_WIDTH = 1024
ROPE_THETA = 500000.0
ROT_DIM = 16

D_FF = 2816
FFN_CONV = 3

IN_SIZES = (SSM_D_INNER, SSM_CONV_DIM, SSM_HEADS, ATT_WIDTH, ATT_WIDTH, ATT_WIDTH, D_MODEL, D_MODEL)

PROJ_WIDTH = 10240
COL_Z, COL_Q, COL_K, COL_V, COL_GS, COL_XBC, COL_GA = 0, 2048, 3072, 4096, 5120, 6144, 9216

LANES = 128
NEG_BIG = -1e30
VMEM_LIMIT_MB = 56


def _cparams(semantics):
    return pltpu.CompilerParams(dimension_semantics=semantics,
                                vmem_limit_bytes=VMEM_LIMIT_MB * 1024 * 1024)


def _sigmoid(x):
    return 1.0 / (1.0 + jnp.exp(-x))


def _silu(x):
    return x * _sigmoid(x)


def _softplus(x):
    return jnp.maximum(x, 0.0) + jnp.log(1.0 + jnp.exp(-jnp.abs(x)))


def _inproj_kernel(x_ref, nw_ref, w_ref, wdt_ref, proj_ref, dt_ref, xn_ref):
    @pl.when(pl.program_id(1) == 0)
    def _():
        x = x_ref[...]
        xn = x * lax.rsqrt(jnp.mean(x * x, axis=-1, keepdims=True) + NORM_EPS) * nw_ref[...]
        xn_ref[...] = xn.astype(BF16)
        dt_ref[...] = jnp.dot(xn_ref[...], wdt_ref[...], preferred_element_type=F32)

    proj_ref[...] = jnp.dot(xn_ref[...], w_ref[...], preferred_element_type=F32)


def _in_proj(x2d, norm_w, w_main, w_dt):
    n = x2d.shape[0]
    tm = min(n, 1024)
    tn = 2048
    return pl.pallas_call(
        _inproj_kernel,
        grid=(n // tm, PROJ_WIDTH // tn),
        in_specs=[
            pl.BlockSpec((tm, D_MODEL), lambda i, j: (i, 0)),
            pl.BlockSpec((1, D_MODEL), lambda i, j: (0, 0)),
            pl.BlockSpec((D_MODEL, tn), lambda i, j: (0, j)),
            pl.BlockSpec((D_MODEL, LANES), lambda i, j: (0, 0)),
        ],
        out_specs=[
            pl.BlockSpec((tm, tn), lambda i, j: (i, j)),
            pl.BlockSpec((tm, LANES), lambda i, j: (i, 0)),
        ],
        out_shape=[
            jax.ShapeDtypeStruct((n, PROJ_WIDTH), F32),
            jax.ShapeDtypeStruct((n, LANES), F32),
        ],
        scratch_shapes=[pltpu.VMEM((tm, D_MODEL), BF16)],
        compiler_params=_cparams(("parallel", "arbitrary")),
        name="in_proj",
    )(x2d, norm_w, w_main, w_dt)


def _qkprep_kernel(q_ref, k_ref, v_ref, cos_ref, sn_ref, sp_ref, qw_ref, kw_ref, bd_ref,
                   qb_ref, kf_ref, kb_ref, vf_ref, vb_ref, *, q_scale, head_major):
    bd = bd_ref[...]
    cos, sn, sp = cos_ref[...], sn_ref[...], sp_ref[...]

    def norm_rope(x, w):
        x2 = x * x
        hi = x2.astype(BF16)
        lo = (x2 - hi.astype(F32)).astype(BF16)
        ss = (jnp.dot(hi, bd, preferred_element_type=F32)
              + jnp.dot(lo, bd, preferred_element_type=F32))
        y = x * lax.rsqrt(ss * (1.0 / ATT_HEAD_DIM) + NORM_EPS) * w
        return (y * cos + pltpu.roll(y, LANES - ROT_DIM // 2, 1) * sn
                + pltpu.roll(y, ROT_DIM // 2, 1) * sp)

    for g in range(ATT_WIDTH // LANES):
        sl = slice(g * LANES, (g + 1) * LANES)
        qr = norm_rope(q_ref[:, sl], qw_ref[...])
        qb_ref[:, sl] = (qr * q_scale).astype(BF16)
        kr = norm_rope(k_ref[:, sl], kw_ref[...])
        kf_ref[:, sl] = kr
        v = v_ref[:, sl]
        vf_ref[:, sl] = v
        if head_major:
            kb_ref[0, g] = kr.astype(BF16)
            vb_ref[0, g] = v.T.astype(BF16)
        else:
            kb_ref[:, sl] = kr.astype(BF16)
            vb_ref[:, sl] = v.astype(BF16)


def _qk_prep(proj, tables, qw, kw, bd, t, head_major):
    n = proj.shape[0]
    tm = min(t, 512)
    nt = t // tm
    cos, sn, sp = tables
    wide = lambda c: pl.BlockSpec((tm, ATT_WIDTH), lambda i: (i, c // ATT_WIDTH))
    tab = pl.BlockSpec((tm, LANES), lambda i: (i % nt, 0))
    vec = pl.BlockSpec((1, LANES), lambda i: (0, 0))
    out = pl.BlockSpec((tm, ATT_WIDTH), lambda i: (i, 0))
    if head_major:
        kb_spec = pl.BlockSpec((1, ATT_HEADS, tm, LANES), lambda i: (i // nt, 0, i % nt, 0))
        kb_shape = jax.ShapeDtypeStruct((n // t, ATT_HEADS, t, LANES), BF16)
        vb_spec = pl.BlockSpec((1, ATT_HEADS, LANES, tm), lambda i: (i // nt, 0, 0, i % nt))
        vb_shape = jax.ShapeDtypeStruct((n // t, ATT_HEADS, LANES, t), BF16)
    else:
        kb_spec = vb_spec = out
        kb_shape = vb_shape = jax.ShapeDtypeStruct((n, ATT_WIDTH), BF16)
    q_scale = ATT_HEAD_DIM ** -0.5 * math.log2(math.e)
    return pl.pallas_call(
        functools.partial(_qkprep_kernel, q_scale=q_scale, head_major=head_major),
        grid=(n // tm,),
        in_specs=[wide(COL_Q), wide(COL_K), wide(COL_V), tab, tab, tab, vec, vec,
                  pl.BlockSpec((LANES, LANES), lambda i: (0, 0))],
        out_specs=[out, out, kb_spec, out, vb_spec],
        out_shape=[
            jax.ShapeDtypeStruct((n, ATT_WIDTH), BF16),
            jax.ShapeDtypeStruct((n, ATT_WIDTH), F32),
            kb_shape,
            jax.ShapeDtypeStruct((n, ATT_WIDTH), F32),
            vb_shape,
        ],
        compiler_params=_cparams(("parallel",)),
        name="qk_prep",
    )(proj, proj, proj, cos, sn, sp, qw, kw, bd)


def _rope_tables(t, past_len):
    half = ROT_DIM // 2
    inv_freq = jnp.power(ROPE_THETA, -jnp.arange(half, dtype=F32) * 2.0 / ROT_DIM)
    pos = past_len + jnp.arange(t, dtype=jnp.int32)
    ang = pos.astype(F32)[:, None] * inv_freq[None, :]
    cos, sin = jnp.cos(ang), jnp.sin(ang)
    ones = jnp.ones((t, ATT_HEAD_DIM - ROT_DIM), F32)
    zeros_h = jnp.zeros((t, half), F32)
    zeros_r = jnp.zeros((t, ATT_HEAD_DIM - ROT_DIM), F32)
    cos_h = jnp.concatenate([cos, cos, ones], axis=1)
    sn_h = jnp.concatenate([-sin, zeros_h, zeros_r], axis=1)
    sp_h = jnp.concatenate([zeros_h, sin, zeros_r], axis=1)
    two = lambda a: jnp.concatenate([a, a], axis=1)
    return two(cos_h), two(sn_h), two(sp_h)


def _lam_kernel(q1_ref, k1_ref, q2_ref, k2_ref, out_ref, *, lambda_init):
    a = jnp.sum(q1_ref[...] * k1_ref[...], axis=-1, keepdims=True)
    b = jnp.sum(q2_ref[...] * k2_ref[...], axis=-1, keepdims=True)
    out_ref[...] = jnp.exp(a) - jnp.exp(b) + lambda_init


def _lambda(q1, k1, q2, k2, lambda_init):
    return pl.pallas_call(
        functools.partial(_lam_kernel, lambda_init=lambda_init),
        out_shape=jax.ShapeDtypeStruct((1, 1), F32),
        name="lambda",
    )(q1, k1, q2, k2)


def _attn_kernel(lam_ref, q_ref, k_ref, v_ref, sw_ref, o_ref, qm_ref, m_ref, l_ref, acc_ref,
                 *, tq, tk, past_len, out_scale):
    i = pl.program_id(1)
    j = pl.program_id(2)
    q0 = past_len + i * tq
    j_last = (((q0 + tq - 1) // CHUNK) * CHUNK + CHUNK - 1) // tk
    needs_mask = (j * tk + tk - 1) // CHUNK > q0 // CHUNK

    @pl.when(j == 0)
    def _():
        lo_half = lax.broadcasted_iota(jnp.int32, (tq, LANES), 1) < ATT_HEAD_DIM
        for h in range(ATT_HEADS):
            qh = q_ref[0, :, h * LANES:(h + 1) * LANES]
            zero = jnp.zeros_like(qh)
            qm_ref[h, 0:tq, :] = jnp.where(lo_half, qh, zero)
            qm_ref[h, tq:2 * tq, :] = jnp.where(lo_half, zero, qh)
        m_ref[...] = jnp.full(m_ref.shape, NEG_BIG, F32)
        l_ref[...] = jnp.zeros(l_ref.shape, F32)
        acc_ref[...] = jnp.zeros(acc_ref.shape, F32)

    def step(masked):
        if masked:
            row = lax.broadcasted_iota(jnp.int32, (2 * tq, tk), 0)
            col = lax.broadcasted_iota(jnp.int32, (2 * tq, tk), 1)
            visible = ((j * tk + col) >> 6) <= ((q0 + (row & (tq - 1))) >> 6)
        for h in range(ATT_HEADS):
            hs = slice(h * LANES, (h + 1) * LANES)
            s = lax.dot_general(qm_ref[h], k_ref[0, :, hs], (((1,), (1,)), ((), ())),
                                preferred_element_type=F32)
            if masked:
                s = jnp.where(visible, s, NEG_BIG)
            m_prev = m_ref[h]
            m_new = jnp.maximum(m_prev, jnp.max(s, axis=-1, keepdims=True))
            alpha = jnp.exp2(m_prev - m_new)
            p = jnp.exp2(s - jnp.concatenate([m_new] * (tk // LANES), axis=1))
            l_ref[h] = alpha * l_ref[h] + jnp.sum(p, axis=-1, keepdims=True)
            acc_ref[h] = acc_ref[h] * alpha + jnp.dot(p.astype(BF16), v_ref[0, :, hs],
                                                      preferred_element_type=F32)
            m_ref[h] = m_new

    @pl.when(jnp.logical_and(j <= j_last, needs_mask))
    def _():
        step(True)

    @pl.when(jnp.logical_and(j <= j_last, jnp.logical_not(needs_mask)))
    def _():
        step(False)

    @pl.when(j == j_last)
    def _():
        lam = lam_ref[0, 0]
        for h in range(ATT_HEADS):
            hs = slice(h * LANES, (h + 1) * LANES)
            o = acc_ref[h] / l_ref[h]
            o = o[0:tq] - lam * o[tq:2 * tq]
            on = o * lax.rsqrt(jnp.mean(o * o, axis=-1, keepdims=True) + NORM_EPS) * sw_ref[...]
            o_ref[0, :, hs] = (on * out_scale).astype(BF16)


def _key_block(s_len):
    return min(512, -(-s_len // LANES) * LANES)


def _attention(lam, q, k, v, subln_w, past_len, out_scale):
    b, t, _ = q.shape
    s_len = k.shape[1]
    tq = min(t, 256)
    tk = _key_block(s_len)
    assert tq & (tq - 1) == 0 and CHUNK == 64 and s_len % tk == 0 and t % tq == 0 and tk % LANES == 0
    nq, nk = t // tq, s_len // tk

    def kv_map(bi, i, j):
        j_last = (((past_len + i * tq + tq - 1) // CHUNK) * CHUNK + CHUNK - 1) // tk
        return (bi, jnp.minimum(j, j_last), 0)

    return pl.pallas_call(
        functools.partial(_attn_kernel, tq=tq, tk=tk, past_len=past_len, out_scale=out_scale),
        grid=(b, nq, nk),
        in_specs=[
            pl.BlockSpec(memory_space=pltpu.SMEM),
            pl.BlockSpec((1, tq, ATT_WIDTH), lambda bi, i, j: (bi, i, 0)),
            pl.BlockSpec((1, tk, ATT_WIDTH), kv_map),
            pl.BlockSpec((1, tk, ATT_WIDTH), kv_map),
            pl.BlockSpec((1, LANES), lambda bi, i, j: (0, 0)),
        ],
        out_specs=pl.BlockSpec((1, tq, ATT_WIDTH), lambda bi, i, j: (bi, i, 0)),
        out_shape=jax.ShapeDtypeStruct((b, t, ATT_WIDTH), BF16),
        scratch_shapes=[
            pltpu.VMEM((ATT_HEADS, 2 * tq, LANES), BF16),
            pltpu.VMEM((ATT_HEADS, 2 * tq, LANES), F32),
            pltpu.VMEM((ATT_HEADS, 2 * tq, LANES), F32),
            pltpu.VMEM((ATT_HEADS, 2 * tq, LANES), F32),
        ],
        compiler_params=_cparams(("parallel", "parallel", "arbitrary")),
        name="attention",
    )(lam, q, k, v, subln_w)


def _attn_t_kernel(lam_ref, q_ref, k_ref, vt_ref, swb_ref, o_ref, qm_ref, m_ref, l_ref, acc_ref, s_ref, bmax_ref,
                   *, tq, tk, past_len, out_scale):
    i = pl.program_id(1)
    j = pl.program_id(2)
    q0 = past_len + i * tq
    j_last = (((q0 + tq - 1) // CHUNK) * CHUNK + CHUNK - 1) // tk
    col_tile = 2 * LANES

    @pl.when(j == 0)
    def _():
        lo_half = lax.broadcasted_iota(jnp.int32, (tq, LANES), 1) < ATT_HEAD_DIM
        for h in range(ATT_HEADS):
            qh = q_ref[0, :, h * LANES:(h + 1) * LANES]
            zero = jnp.zeros_like(qh)
            qm_ref[h, 0:tq, :] = jnp.where(lo_half, qh, zero)
            qm_ref[h, tq:2 * tq, :] = jnp.where(lo_half, zero, qh)
        m_ref[...] = jnp.full(m_ref.shape, NEG_BIG, F32)
        l_ref[...] = jnp.zeros(l_ref.shape, F32)
        acc_ref[...] = jnp.zeros(acc_ref.shape, F32)

    def block(masked):
        if masked:
            k_chunk = (j * tk + lax.broadcasted_iota(jnp.int32, (tk, 2 * tq), 0)) >> 6
            q_col = lax.broadcasted_iota(jnp.int32, (tk, 2 * tq), 1) & (tq - 1)
            visible = k_chunk <= ((q0 + q_col) >> 6)

        def scores(h, slot):
            st = lax.dot_general(k_ref[0, h], qm_ref[h], (((1,), (1,)), ((), ())),
                                 preferred_element_type=F32)
            if masked:
                st = jnp.where(visible, st, NEG_BIG)
            s_ref[slot] = st
            bmax_ref[slot] = jnp.max(st, axis=0, keepdims=True)

        def update(h, slot):
            m_prev = m_ref[h]
            m_new = jnp.maximum(m_prev, bmax_ref[slot])
            alpha = jnp.exp2(m_prev - m_new)
            m_ref[h] = m_new
            sums, pvs = [], []
            for c in range(2 * tq // col_tile):
                cs = slice(c * col_tile, (c + 1) * col_tile)
                p = jnp.exp2(s_ref[slot, :, cs] - m_new[:, cs])
                sums.append(jnp.sum(p, axis=0, keepdims=True))
                pvs.append(jnp.dot(vt_ref[0, h], p.astype(BF16), preferred_element_type=F32))
            l_ref[h] = alpha * l_ref[h] + jnp.concatenate(sums, axis=1)
            acc_ref[h] = acc_ref[h] * alpha + jnp.concatenate(pvs, axis=1)

        scores(0, 0)

        def body(g, carry):
            h = 2 * g
            scores(h + 1, 1)
            update(h, 0)
            scores(h + 2, 0)
            update(h + 1, 1)
            return carry

        lax.fori_loop(0, ATT_HEADS // 2 - 1, body, 0)
        scores(ATT_HEADS - 1, 1)
        update(ATT_HEADS - 2, 0)
        update(ATT_HEADS - 1, 1)

    needs_mask = (j * tk + tk - 1) // CHUNK > q0 // CHUNK
    pl.when(jnp.logical_and(j <= j_last, needs_mask))(functools.partial(block, True))
    pl.when(jnp.logical_and(j <= j_last, jnp.logical_not(needs_mask)))(functools.partial(block, False))

    @pl.when(j == j_last)
    def _():
        lam = lam_ref[0, 0]
        swb = jnp.concatenate([swb_ref[...]] * (tq // LANES), axis=1)
        for h in range(ATT_HEADS):
            o = acc_ref[h] / l_ref[h]
            o = o[:, 0:tq] - lam * o[:, tq:2 * tq]
            on = o * lax.rsqrt(jnp.mean(o * o, axis=0, keepdims=True) + NORM_EPS) * swb
            o_ref[0, :, h * LANES:(h + 1) * LANES] = (on * out_scale).T.astype(BF16)


def _attention_t(lam, q, k, vt, subln_w, past_len, out_scale):
    b, t, _ = q.shape
    s_len = k.shape[2]
    tq = min(t, 512)
    tk = min(s_len, 512)
    assert tq & (tq - 1) == 0 and tq >= 2 * LANES and s_len % tk == 0 and t % tq == 0 and tk % LANES == 0
    nq, nk = t // tq, s_len // tk

    def j_last(i):
        return (((past_len + i * tq + tq - 1) // CHUNK) * CHUNK + CHUNK - 1) // tk

    swb = jnp.broadcast_to(subln_w.reshape(LANES, 1), (LANES, LANES))
    return pl.pallas_call(
        functools.partial(_attn_t_kernel, tq=tq, tk=tk, past_len=past_len, out_scale=out_scale),
        grid=(b, nq, nk),
        in_specs=[
            pl.BlockSpec(memory_space=pltpu.SMEM),
            pl.BlockSpec((1, tq, ATT_WIDTH), lambda bi, i, j: (bi, i, 0)),
            pl.BlockSpec((1, ATT_HEADS, tk, LANES), lambda bi, i, j: (bi, 0, jnp.minimum(j, j_last(i)), 0)),
            pl.BlockSpec((1, ATT_HEADS, LANES, tk), lambda bi, i, j: (bi, 0, 0, jnp.minimum(j, j_last(i)))),
            pl.BlockSpec((LANES, LANES), lambda bi, i, j: (0, 0)),
        ],
        out_specs=pl.BlockSpec((1, tq, ATT_WIDTH), lambda bi, i, j: (bi, i, 0)),
        out_shape=jax.ShapeDtypeStruct((b, t, ATT_WIDTH), BF16),
        scratch_shapes=[
            pltpu.VMEM((ATT_HEADS, 2 * tq, LANES), BF16),
            pltpu.VMEM((ATT_HEADS, 1, 2 * tq), F32),
            pltpu.VMEM((ATT_HEADS, 1, 2 * tq), F32),
            pltpu.VMEM((ATT_HEADS, LANES, 2 * tq), F32),
            pltpu.VMEM((2, tk, 2 * tq), F32),
            pltpu.VMEM((2, 1, 2 * tq), F32),
        ],
        compiler_params=_cparams(("parallel", "parallel", "arbitrary")),
        name="attention_t",
    )(lam, q, k, vt, swb)


def _ssd_kernel(xbc_ref, z_ref, dt_ref, cw_ref, cb_ref, dtb_ref, alog_ref, dexp_ref, nw_ref, tril_ref,
                cpast_ref, spast_ref, y_ref, st_ref, xpad_ref, xcv_ref, state_ref, *, tt):
    ti = pl.program_id(1)
    L = CHUNK
    hist = SSM_CONV - 1
    base = 8

    @pl.when(ti == 0)
    def _():
        xpad_ref[base - hist:base, :] = cpast_ref[0]
        state_ref[...] = spast_ref[0]

    xpad_ref[base:base + tt, :] = xbc_ref[0]
    for c in range(tt // L):
        r = base + c * L
        acc = cb_ref[...] + cw_ref[hist:hist + 1, :] * xpad_ref[r:r + L, :]
        for d in range(1, SSM_CONV):
            acc = acc + cw_ref[hist - d:hist - d + 1, :] * xpad_ref[r - d:r - d + L, :]
        xcv_ref[c * L:(c + 1) * L, :] = _silu(acc)
    xpad_ref[base - hist:base, :] = xpad_ref[base + tt - hist:base + tt, :]

    lane = lax.broadcasted_iota(jnp.int32, (L, LANES), 1)
    lo_half = lane < SSM_HEAD_DIM
    tril2 = (lane & (SSM_HEAD_DIM - 1)) <= lax.broadcasted_iota(jnp.int32, (L, LANES), 0)
    a_neg = -jnp.exp(alog_ref[...])
    tril = tril_ref[...]
    b_off = SSM_D_INNER
    c_off = SSM_D_INNER + SSM_GROUPS * SSM_D_STATE

    def chunk(c, carry):
        r0 = pl.multiple_of(c * L, L)
        rows = pl.ds(r0, L)
        dtv = _softplus(dt_ref[0, rows, :] + dtb_ref[...])
        da = dtv * a_neg
        hi = da.astype(BF16)
        r1 = da - hi.astype(F32)
        mid = r1.astype(BF16)
        lo = (r1 - mid.astype(F32)).astype(BF16)
        acum = (jnp.dot(tril, hi, preferred_element_type=F32)
                + jnp.dot(tril, mid, preferred_element_type=F32)
                + jnp.dot(tril, lo, preferred_element_type=F32))
        acum_t = jnp.concatenate([acum, acum], axis=0).T
        for g in range(SSM_GROUPS):
            gs = slice(g * SSM_GROUP_WIDTH, (g + 1) * SSM_GROUP_WIDTH)
            bm = xcv_ref[rows, b_off + g * SSM_D_STATE:b_off + (g + 1) * SSM_D_STATE].astype(BF16)
            cm = xcv_ref[rows, c_off + g * SSM_D_STATE:c_off + (g + 1) * SSM_D_STATE].astype(BF16)
            cb2 = lax.dot_general(cm, jnp.concatenate([bm, bm], axis=0), (((1,), (1,)), ((), ())),
                                  preferred_element_type=F32)
            st = state_ref[g]
            y_in = jnp.dot(cm, st.astype(BF16), preferred_element_type=F32)
            ys, xws, lasts = [], [], []
            for pp in range(SSM_PAIRS_PER_GROUP):
                pair = g * SSM_PAIRS_PER_GROUP + pp
                h1, h2 = 2 * pair, 2 * pair + 1
                ps = slice(pair * LANES, (pair + 1) * LANES)
                a_col = jnp.where(lo_half, acum[:, h1:h1 + 1], acum[:, h2:h2 + 1])
                dt_col = jnp.where(lo_half, dtv[:, h1:h1 + 1], dtv[:, h2:h2 + 1])
                a_row = jnp.where(lo_half[0:1], acum_t[h1:h1 + 1, :], acum_t[h2:h2 + 1, :])
                last = a_col[L - 1:L, :]
                xs = xcv_ref[rows, ps]
                xdt = xs * dt_col
                w = (cb2 * jnp.exp(jnp.where(tril2, a_col - a_row, NEG_BIG))).astype(BF16)
                zero = jnp.zeros_like(xdt)
                rhs = jnp.concatenate([jnp.where(lo_half, xdt, zero), jnp.where(lo_half, zero, xdt)],
                                      axis=0).astype(BF16)
                y = (jnp.dot(w, rhs, preferred_element_type=F32)
                     + y_in[:, pp * LANES:(pp + 1) * LANES] * jnp.exp(a_col)
                     + dexp_ref[:, ps] * xs)
                ys.append(y)
                xws.append(xdt * jnp.exp(last - a_col))
                lasts.append(last)
            yg = jnp.concatenate(ys, axis=1)
            xwg = jnp.concatenate(xws, axis=1).astype(BF16)
            lastg = jnp.concatenate(lasts, axis=1)
            state_ref[g] = st * jnp.exp(lastg) + lax.dot_general(
                bm, xwg, (((0,), (0,)), ((), ())), preferred_element_type=F32)
            yg = yg * _silu(z_ref[0, rows, gs])
            yn = yg * lax.rsqrt(jnp.mean(yg * yg, axis=-1, keepdims=True) + NORM_EPS) * nw_ref[:, gs]
            y_ref[0, rows, gs] = yn.astype(BF16)
        return carry

    lax.fori_loop(0, tt // L, chunk, 0, unroll=True)

    @pl.when(ti == pl.num_programs(1) - 1)
    def _():
        st_ref[0] = state_ref[...]


def _ssd(proj3, dt3, conv_past, state_past_t, cw, cb, dtb, alog, dexp, nw, tril):
    b, t, _ = proj3.shape
    tt = min(t, 256)
    full = lambda shape: pl.BlockSpec(shape, lambda bi, i: (0,) * len(shape))
    return pl.pallas_call(
        functools.partial(_ssd_kernel, tt=tt),
        grid=(b, t // tt),
        in_specs=[
            pl.BlockSpec((1, tt, SSM_CONV_DIM), lambda bi, i: (bi, i, COL_XBC // SSM_CONV_DIM)),
            pl.BlockSpec((1, tt, SSM_D_INNER), lambda bi, i: (bi, i, COL_Z // SSM_D_INNER)),
            pl.BlockSpec((1, tt, LANES), lambda bi, i: (bi, i, 0)),
            full((SSM_CONV, SSM_CONV_DIM)), full((1, SSM_CONV_DIM)), full((1, LANES)), full((1, LANES)),
            full((1, SSM_D_INNER)), full((1, SSM_D_INNER)), full((CHUNK, CHUNK)),
            pl.BlockSpec((1, SSM_CONV - 1, SSM_CONV_DIM), lambda bi, i: (bi, 0, 0)),
            pl.BlockSpec((1, SSM_GROUPS, SSM_D_STATE, SSM_GROUP_WIDTH), lambda bi, i: (bi, 0, 0, 0)),
        ],
        out_specs=[
            pl.BlockSpec((1, tt, SSM_D_INNER), lambda bi, i: (bi, i, 0)),
            pl.BlockSpec((1, SSM_GROUPS, SSM_D_STATE, SSM_GROUP_WIDTH), lambda bi, i: (bi, 0, 0, 0)),
        ],
        out_shape=[
            jax.ShapeDtypeStruct((b, t, SSM_D_INNER), BF16),
            jax.ShapeDtypeStruct((b, SSM_GROUPS, SSM_D_STATE, SSM_GROUP_WIDTH), F32),
        ],
        scratch_shapes=[
            pltpu.VMEM((tt + 8, SSM_CONV_DIM), F32),
            pltpu.VMEM((tt, SSM_CONV_DIM), F32),
            pltpu.VMEM((SSM_GROUPS, SSM_D_STATE, SSM_GROUP_WIDTH), F32),
        ],
        compiler_params=_cparams(("parallel", "arbitrary")),
        name="ssd",
    )(proj3, proj3, dt3, cw, cb, dtb, alog, dexp, nw, tril, conv_past, state_past_t)


def _merge_kernel(x_ref, ys_ref, o_ref, gs_ref, ga_ref, wbs_ref, wba_ref, wo_ref, nfw_ref, x1_ref, hn_ref):
    bs = jnp.dot(ys_ref[...], wbs_ref[...], preferred_element_type=F32)
    ba = jnp.dot(o_ref[...], wba_ref[...], preferred_element_type=F32)
    mix = _sigmoid(gs_ref[...]) * bs + _sigmoid(ga_ref[...]) * ba
    x1 = x_ref[...] + jnp.dot(mix.astype(BF16), wo_ref[...], preferred_element_type=F32)
    x1_ref[...] = x1
    hn = x1 * lax.rsqrt(jnp.mean(x1 * x1, axis=-1, keepdims=True) + NORM_EPS) * nfw_ref[...]
    hn_ref[...] = hn.astype(BF16)


def _merge(x2d, ys, o, proj, wbs, wba, wo, nfw):
    n = x2d.shape[0]
    tm = min(n, 512)
    row = lambda w, c=0: pl.BlockSpec((tm, w), lambda i: (i, c))
    full = lambda shape: pl.BlockSpec(shape, lambda i: (0, 0))
    return pl.pallas_call(
        _merge_kernel,
        grid=(n // tm,),
        in_specs=[row(D_MODEL), row(SSM_D_INNER), row(ATT_WIDTH),
                  row(D_MODEL, COL_GS // D_MODEL), row(D_MODEL, COL_GA // D_MODEL),
                  full((SSM_D_INNER, D_MODEL)), full((ATT_WIDTH, D_MODEL)), full((D_MODEL, D_MODEL)),
                  full((1, D_MODEL))],
        out_specs=[row(D_MODEL), row(D_MODEL)],
        out_shape=[jax.ShapeDtypeStruct((n, D_MODEL), F32), jax.ShapeDtypeStruct((n, D_MODEL), BF16)],
        compiler_params=_cparams(("parallel",)),
        name="merge",
    )(x2d, ys, o, proj, proj, wbs, wba, wo, nfw)


FFN_SPLIT = 2


def _ffn_kernel(hn_ref, x1_ref, wa_ref, wb_ref, wd_ref, cw_ref, cb_ref, past_ref, out_ref, tail_ref,
                hbuf_ref, *, tm):
    ti = pl.program_id(1)
    hist = FFN_CONV - 1
    base = 8
    tf = D_FF // FFN_SPLIT

    @pl.when(ti == 0)
    def _():
        hbuf_ref[base - hist:base, :] = past_ref[0]

    hn = hn_ref[0]
    acc = x1_ref[0]
    for f in range(FFN_SPLIT):
        fs = slice(f * tf, (f + 1) * tf)
        ha = jnp.dot(hn, wa_ref[:, fs], preferred_element_type=F32)
        hb = jnp.dot(hn, wb_ref[:, fs], preferred_element_type=F32)
        hbuf_ref[base:base + tm, fs] = ha
        hc = cb_ref[:, fs] + cw_ref[hist:hist + 1, fs] * ha
        for d in range(1, FFN_CONV):
            hc = hc + cw_ref[hist - d:hist - d + 1, fs] * hbuf_ref[base - d:base - d + tm, fs]
        u = (_silu(hc) * hb).astype(BF16)
        acc = acc + jnp.dot(u, wd_ref[fs, :], preferred_element_type=F32)
    out_ref[0] = acc
    tail = hbuf_ref[base + tm - hist:base + tm, :]
    hbuf_ref[base - hist:base, :] = tail
    tail_ref[0] = tail


def _ffn(hn3, x13, wa, wb, wd, cw, cb, past):
    b, t, _ = hn3.shape
    tm = min(t, 512)
    const = lambda shape: pl.BlockSpec(shape, lambda bi, i: (0, 0), pipeline_mode=pl.Buffered(1))
    return pl.pallas_call(
        functools.partial(_ffn_kernel, tm=tm),
        grid=(b, t // tm),
        in_specs=[
            pl.BlockSpec((1, tm, D_MODEL), lambda bi, i: (bi, i, 0)),
            pl.BlockSpec((1, tm, D_MODEL), lambda bi, i: (bi, i, 0)),
            const((D_MODEL, D_FF)), const((D_MODEL, D_FF)), const((D_FF, D_MODEL)),
            const((FFN_CONV, D_FF)), const((1, D_FF)),
            pl.BlockSpec((1, FFN_CONV - 1, D_FF), lambda bi, i: (bi, 0, 0)),
        ],
        out_specs=[
            pl.BlockSpec((1, tm, D_MODEL), lambda bi, i: (bi, i, 0)),
            pl.BlockSpec((1, FFN_CONV - 1, D_FF), lambda bi, i: (bi, 0, 0)),
        ],
        out_shape=[
            jax.ShapeDtypeStruct((b, t, D_MODEL), F32),
            jax.ShapeDtypeStruct((b, FFN_CONV - 1, D_FF), F32),
        ],
        scratch_shapes=[pltpu.VMEM((tm + 8, D_FF), F32)],
        compiler_params=_cparams(("parallel", "arbitrary")),
        name="ffn",
    )(hn3, x13, wa, wb, wd, cw, cb, past)


def _prepare_weights(norm_mix_w, w_in, ssm_conv_w, ssm_conv_b, ssm_dt_bias, ssm_a_log, ssm_d, ssm_norm_w,
                     q_norm_w, k_norm_w, subln_w, w_branch_ssm, w_branch_attn, w_out, norm_ffn_w, w_up,
                     ffn_conv_w, ffn_conv_b, w_down):
    bounds = [0]
    for s in IN_SIZES:
        bounds.append(bounds[-1] + s)
    piece = lambda n: w_in[:, bounds[n]:bounds[n + 1]]
    z, xbc, dt, q, k, v, gs, ga = (piece(n) for n in range(8))
    pad_heads = lambda a: jnp.pad(a.reshape(1, SSM_HEADS), ((0, 0), (0, LANES - SSM_HEADS)))
    row = lambda a: a.reshape(1, -1)
    two_heads = lambda a: jnp.concatenate([a, a]).reshape(1, LANES)
    head = jnp.arange(LANES) // ATT_HEAD_DIM
    tri = jnp.arange(CHUNK)
    return dict(
        norm_mix=row(norm_mix_w),
        w_main=jnp.concatenate([z, q, k, v, gs, xbc, ga], axis=1).astype(BF16),
        w_dt=jnp.pad(dt, ((0, 0), (0, LANES - SSM_HEADS))).astype(BF16),
        conv_w=ssm_conv_w, conv_b=row(ssm_conv_b),
        dt_bias=pad_heads(ssm_dt_bias), a_log=pad_heads(ssm_a_log),
        d_exp=row(jnp.repeat(ssm_d, SSM_HEAD_DIM)), ssm_norm=row(ssm_norm_w),
        tril=(tri[:, None] >= tri[None, :]).astype(BF16),
        q_norm=two_heads(q_norm_w), k_norm=two_heads(k_norm_w),
        head_ones=(head[:, None] == head[None, :]).astype(BF16),
        subln=row(subln_w),
        w_bs=w_branch_ssm.astype(BF16), w_ba=w_branch_attn.astype(BF16), w_out=w_out.astype(BF16),
        norm_ffn=row(norm_ffn_w),
        w_up_a=w_up[:, :D_FF].astype(BF16), w_up_b=w_up[:, D_FF:].astype(BF16),
        w_down=w_down.astype(BF16), ffn_conv_w=ffn_conv_w, ffn_conv_b=row(ffn_conv_b),
    )


def _state_to_kernel_layout(s):
    b = s.shape[0]
    s = s.reshape(b, SSM_GROUPS, SSM_HEADS // SSM_GROUPS, SSM_HEAD_DIM, SSM_D_STATE)
    return jnp.transpose(s, (0, 1, 4, 2, 3)).reshape(b, SSM_GROUPS, SSM_D_STATE, SSM_GROUP_WIDTH)


def _state_from_kernel_layout(s):
    b = s.shape[0]
    s = s.reshape(b, SSM_GROUPS, SSM_D_STATE, SSM_HEADS // SSM_GROUPS, SSM_HEAD_DIM)
    return jnp.transpose(s, (0, 1, 3, 4, 2)).reshape(b, SSM_HEADS, SSM_HEAD_DIM, SSM_D_STATE)


def _layer(x, k_past, v_past, ssm_past, conv_past, ffn_past, lam, lambda_init, w):
    b, t, _ = x.shape
    past_len = k_past.shape[1]
    n = b * t
    x2d = x.reshape(n, D_MODEL)

    proj, dt = _in_proj(x2d, w["norm_mix"], w["w_main"], w["w_dt"])
    keys_on_rows = past_len == 0 and t % (4 * LANES) == 0
    qb, kf, kb, vf, vb = _qk_prep(proj, _rope_tables(t, past_len), w["q_norm"], w["k_norm"],
                                  w["head_ones"], t, keys_on_rows)
    q3 = qb.reshape(b, t, ATT_WIDTH)
    if keys_on_rows:
        o = _attention_t(lam, q3, kb, vb, w["subln"], past_len, 1.0 - lambda_init)
    else:
        k_all = kb.reshape(b, t, ATT_WIDTH)
        v_all = vb.reshape(b, t, ATT_WIDTH)
        if past_len:
            k_all = jnp.concatenate([k_past.reshape(b, past_len, ATT_WIDTH).astype(BF16), k_all], axis=1)
            v_all = jnp.concatenate([v_past.reshape(b, past_len, ATT_WIDTH).astype(BF16), v_all], axis=1)
        s_len = past_len + t
        pad = (-s_len) % _key_block(s_len)
        if pad:
            k_all = jnp.pad(k_all, ((0, 0), (0, pad), (0, 0)))
            v_all = jnp.pad(v_all, ((0, 0), (0, pad), (0, 0)))
        o = _attention(lam, q3, k_all, v_all, w["subln"], past_len, 1.0 - lambda_init)

    ys, state_t = _ssd(proj.reshape(b, t, PROJ_WIDTH), dt.reshape(b, t, LANES), conv_past,
                       _state_to_kernel_layout(ssm_past), w["conv_w"], w["conv_b"], w["dt_bias"],
                       w["a_log"], w["d_exp"], w["ssm_norm"], w["tril"])

    x1, hn = _merge(x2d, ys.reshape(n, SSM_D_INNER), o.reshape(n, ATT_WIDTH), proj,
                    w["w_bs"], w["w_ba"], w["w_out"], w["norm_ffn"])
    y, ffn_new = _ffn(hn.reshape(b, t, D_MODEL), x1.reshape(b, t, D_MODEL), w["w_up_a"], w["w_up_b"],
                      w["w_down"], w["ffn_conv_w"], w["ffn_conv_b"], ffn_past)

    xbc_rows = proj.reshape(b, t, PROJ_WIDTH)[:, t - (SSM_CONV - 1):, COL_XBC:COL_XBC + SSM_CONV_DIM]
    k_new = kf.reshape(b, t, 2 * ATT_HEADS, ATT_HEAD_DIM)
    v_new = vf.reshape(b, t, ATT_HEADS, 2 * ATT_HEAD_DIM)
    return y, k_new, v_new, _state_from_kernel_layout(state_t), xbc_rows, ffn_new


def kernel(x_prompt, x_sample, cache_k, cache_v, state_ssm, state_ssm_conv, state_ffn_conv, norm_mix_w, w_in, ssm_conv_w, ssm_conv_b, ssm_dt_bias, ssm_a_log, ssm_d, ssm_norm_w, q_norm_w, k_norm_w, lambda_q1, lambda_k1, lambda_q2, lambda_k2, subln_w, w_branch_ssm, w_branch_attn, w_out, norm_ffn_w, w_up, ffn_conv_w, ffn_conv_b, w_down):
    depth = w_in.shape[0]
    assert depth == 1
    bp = x_prompt.shape[0]
    dt_ = x_prompt.dtype
    layer = 0
    lambda_init = 0.8 - 0.6 * math.exp(-0.3 * layer)
    w = _prepare_weights(norm_mix_w[layer], w_in[layer], ssm_conv_w[layer], ssm_conv_b[layer],
                         ssm_dt_bias[layer], ssm_a_log[layer], ssm_d[layer], ssm_norm_w[layer],
                         q_norm_w[layer], k_norm_w[layer], subln_w[layer], w_branch_ssm[layer],
                         w_branch_attn[layer], w_out[layer], norm_ffn_w[layer], w_up[layer],
                         ffn_conv_w[layer], ffn_conv_b[layer], w_down[layer])
    lam = _lambda(lambda_q1[layer].reshape(1, -1), lambda_k1[layer].reshape(1, -1),
                  lambda_q2[layer].reshape(1, -1), lambda_k2[layer].reshape(1, -1), lambda_init)

    yp, kp, vp, sp, cp, fp = _layer(
        x_prompt,
        jnp.zeros((bp, 0, 2 * ATT_HEADS, ATT_HEAD_DIM), dt_),
        jnp.zeros((bp, 0, ATT_HEADS, 2 * ATT_HEAD_DIM), dt_),
        jnp.zeros((bp, SSM_HEADS, SSM_HEAD_DIM, SSM_D_STATE), dt_),
        jnp.zeros((bp, SSM_CONV - 1, SSM_CONV_DIM), dt_),
        jnp.zeros((bp, FFN_CONV - 1, D_FF), dt_),
        lam, lambda_init, w)
    ys, ks, vs, ss, cs, fs = _layer(
        x_sample, cache_k[layer], cache_v[layer], state_ssm[layer], state_ssm_conv[layer],
        state_ffn_conv[layer], lam, lambda_init, w)
    stack = lambda a: a[None]
    return (yp, ys, stack(kp), stack(vp), stack(sp), stack(cp), stack(fp),
            stack(ks), stack(vs), stack(ss), stack(cs), stack(fs))
```

```python
import functools
import math

import jax
import jax.numpy as jnp
from jax import lax
from jax.experimental import pallas as pl
from jax.experimental.pallas import tpu as pltpu

F32 = jnp.float32
BF16 = jnp.bfloat16

D_MODEL = 1024
CHUNK = 64
NORM_EPS = 1e-6

SSM_D_INNER = 2048
SSM_HEAD_DIM = 64
SSM_HEADS = 32
SSM_GROUPS = 4
SSM_D_STATE = 128
SSM_CONV = 4
SSM_CONV_DIM = 3072
SSM_GROUP_WIDTH = SSM_D_INNER // SSM_GROUPS
SSM_PAIRS_PER_GROUP = SSM_GROUP_WIDTH // 128

ATT_HEADS = 8
ATT_HEAD_DIM = 64
ATT_WIDTH = 1024
ROPE_THETA = 500000.0
ROT_DIM = 16

D_FF = 2816
FFN_CONV = 3

IN_SIZES = (SSM_D_INNER, SSM_CONV_DIM, SSM_HEADS, ATT_WIDTH, ATT_WIDTH, ATT_WIDTH, D_MODEL, D_MODEL)

PROJ_WIDTH = 10240
COL_Z, COL_Q, COL_K, COL_V, COL_GS, COL_XBC, COL_GA = 0, 2048, 3072, 4096, 5120, 6144, 9216

LANES = 128
NEG_BIG = -1e30
VMEM_LIMIT_MB = 56


def _cparams(semantics):
    return pltpu.CompilerParams(dimension_semantics=semantics,
                                vmem_limit_bytes=VMEM_LIMIT_MB * 1024 * 1024)


def _sigmoid(x):
    return 1.0 / (1.0 + jnp.exp(-x))


def _silu(x):
    return x * _sigmoid(x)


def _softplus(x):
    return jnp.maximum(x, 0.0) + jnp.log(1.0 + jnp.exp(-jnp.abs(x)))


def _inproj_kernel(x_ref, nw_ref, w_ref, wdt_ref, proj_ref, dt_ref, xn_ref):
    @pl.when(pl.program_id(1) == 0)
    def _():
        x = x_ref[...]
        xn = x * lax.rsqrt(jnp.mean(x * x, axis=-1, keepdims=True) + NORM_EPS) * nw_ref[...]
        xn_ref[...] = xn.astype(BF16)
        dt_ref[...] = jnp.dot(xn_ref[...], wdt_ref[...], preferred_element_type=F32)

    proj_ref[...] = jnp.dot(xn_ref[...], w_ref[...], preferred_element_type=F32)


def _in_proj(x2d, norm_w, w_main, w_dt):
    n = x2d.shape[0]
    tm = min(n, 1024)
    tn = 2048
    return pl.pallas_call(
        _inproj_kernel,
        grid=(n // tm, PROJ_WIDTH // tn),
        in_specs=[
            pl.BlockSpec((tm, D_MODEL), lambda i, j: (i, 0)),
            pl.BlockSpec((1, D_MODEL), lambda i, j: (0, 0)),
            pl.BlockSpec((D_MODEL, tn), lambda i, j: (0, j)),
            pl.BlockSpec((D_MODEL, LANES), lambda i, j: (0, 0)),
        ],
        out_specs=[
            pl.BlockSpec((tm, tn), lambda i, j: (i, j)),
            pl.BlockSpec((tm, LANES), lambda i, j: (i, 0)),
        ],
        out_shape=[
            jax.ShapeDtypeStruct((n, PROJ_WIDTH), F32),
            jax.ShapeDtypeStruct((n, LANES), F32),
        ],
        scratch_shapes=[pltpu.VMEM((tm, D_MODEL), BF16)],
        compiler_params=_cparams(("parallel", "arbitrary")),
        name="in_proj",
    )(x2d, norm_w, w_main, w_dt)


def _qkprep_kernel(q_ref, k_ref, v_ref, cos_ref, sn_ref, sp_ref, qw_ref, kw_ref, bd_ref,
                   qb_ref, kf_ref, kb_ref, vf_ref, vb_ref, *, q_scale, head_major):
    tm = q_ref.shape[0]
    bd = bd_ref[...]
    cos, sn, sp = cos_ref[...], sn_ref[...], sp_ref[...]

    def norm_rope(x, w):
        x2 = x * x
        hi = x2.astype(BF16)
        lo = (x2 - hi.astype(F32)).astype(BF16)
        ss = (jnp.dot(hi, bd, preferred_element_type=F32)
              + jnp.dot(lo, bd, preferred_element_type=F32))
        y = x * lax.rsqrt(ss * (1.0 / ATT_HEAD_DIM) + NORM_EPS) * w
        return (y * cos + pltpu.roll(y, LANES - ROT_DIM // 2, 1) * sn
                + pltpu.roll(y, ROT_DIM // 2, 1) * sp)

    for g in range(ATT_WIDTH // LANES):
        sl = slice(g * LANES, (g + 1) * LANES)
        qr = norm_rope(q_ref[:, sl], qw_ref[...])
        qb_ref[:, sl] = (qr * q_scale).astype(BF16)
        kr = norm_rope(k_ref[:, sl], kw_ref[...])
        kf_ref[:, sl] = kr
        v = v_ref[:, sl]
        vf_ref[pl.ds(g, tm, stride=ATT_HEADS), :] = v
        if head_major:
            kb_ref[0, g] = kr.astype(BF16)
            vb_ref[0, g] = v.T.astype(BF16)
        else:
            kb_ref[:, sl] = kr.astype(BF16)
            vb_ref[:, sl] = v.astype(BF16)


def _qk_prep(proj, tables, qw, kw, bd, t, head_major):
    n = proj.shape[0]
    tm = min(t, 512)
    nt = t // tm
    cos, sn, sp = tables
    wide = lambda c: pl.BlockSpec((tm, ATT_WIDTH), lambda i: (i, c // ATT_WIDTH))
    tab = pl.BlockSpec((tm, LANES), lambda i: (i % nt, 0))
    vec = pl.BlockSpec((1, LANES), lambda i: (0, 0))
    out = pl.BlockSpec((tm, ATT_WIDTH), lambda i: (i, 0))
    if head_major:
        kb_spec = pl.BlockSpec((1, ATT_HEADS, tm, LANES), lambda i: (i // nt, 0, i % nt, 0))
        kb_shape = jax.ShapeDtypeStruct((n // t, ATT_HEADS, t, LANES), BF16)
        vb_spec = pl.BlockSpec((1, ATT_HEADS, LANES, tm), lambda i: (i // nt, 0, 0, i % nt))
        vb_shape = jax.ShapeDtypeStruct((n // t, ATT_HEADS, LANES, t), BF16)
    else:
        kb_spec = vb_spec = out
        kb_shape = vb_shape = jax.ShapeDtypeStruct((n, ATT_WIDTH), BF16)
    q_scale = ATT_HEAD_DIM ** -0.5 * math.log2(math.e)
    return pl.pallas_call(
        functools.partial(_qkprep_kernel, q_scale=q_scale, head_major=head_major),
        grid=(n // tm,),
        in_specs=[wide(COL_Q), wide(COL_K), wide(COL_V), tab, tab, tab, vec, vec,
                  pl.BlockSpec((LANES, LANES), lambda i: (0, 0))],
        out_specs=[out, out, kb_spec,
                   pl.BlockSpec((tm * ATT_HEADS, LANES), lambda i: (i, 0)), vb_spec],
        out_shape=[
            jax.ShapeDtypeStruct((n, ATT_WIDTH), BF16),
            jax.ShapeDtypeStruct((n, ATT_WIDTH), F32),
            kb_shape,
            jax.ShapeDtypeStruct((n * ATT_HEADS, LANES), F32),
            vb_shape,
        ],
        compiler_params=_cparams(("parallel",)),
        name="qk_prep",
    )(proj, proj, proj, cos, sn, sp, qw, kw, bd)


def _rope_tables(t, past_len):
    half = ROT_DIM // 2
    inv_freq = jnp.power(ROPE_THETA, -jnp.arange(half, dtype=F32) * 2.0 / ROT_DIM)
    pos = past_len + jnp.arange(t, dtype=jnp.int32)
    ang = pos.astype(F32)[:, None] * inv_freq[None, :]
    cos, sin = jnp.cos(ang), jnp.sin(ang)
    ones = jnp.ones((t, ATT_HEAD_DIM - ROT_DIM), F32)
    zeros_h = jnp.zeros((t, half), F32)
    zeros_r = jnp.zeros((t, ATT_HEAD_DIM - ROT_DIM), F32)
    cos_h = jnp.concatenate([cos, cos, ones], axis=1)
    sn_h = jnp.concatenate([-sin, zeros_h, zeros_r], axis=1)
    sp_h = jnp.concatenate([zeros_h, sin, zeros_r], axis=1)
    two = lambda a: jnp.concatenate([a, a], axis=1)
    return two(cos_h), two(sn_h), two(sp_h)


def _lam_kernel(q1_ref, k1_ref, q2_ref, k2_ref, out_ref, *, lambda_init):
    a = jnp.sum(q1_ref[...] * k1_ref[...], axis=-1, keepdims=True)
    b = jnp.sum(q2_ref[...] * k2_ref[...], axis=-1, keepdims=True)
    out_ref[...] = jnp.exp(a) - jnp.exp(b) + lambda_init


def _lambda(q1, k1, q2, k2, lambda_init):
    return pl.pallas_call(
        functools.partial(_lam_kernel, lambda_init=lambda_init),
        out_shape=jax.ShapeDtypeStruct((1, 1), F32),
        name="lambda",
    )(q1, k1, q2, k2)


def _attn_kernel(lam_ref, q_ref, kp_ref, vp_ref, kn_ref, vn_ref, sw_ref, o_ref, qm_ref, m_ref, l_ref, acc_ref,
                 *, tq, tn, past_len, out_scale):
    j = pl.program_id(1)
    n_past = pl.num_programs(1) - 1

    @pl.when(j == 0)
    def _():
        lo_half = lax.broadcasted_iota(jnp.int32, (tq, LANES), 1) < ATT_HEAD_DIM
        for h in range(ATT_HEADS):
            qh = q_ref[0, :, h * LANES:(h + 1) * LANES]
            zero = jnp.zeros_like(qh)
            qm_ref[h, 0:tq, :] = jnp.where(lo_half, qh, zero)
            qm_ref[h, tq:2 * tq, :] = jnp.where(lo_half, zero, qh)
        m_ref[...] = jnp.full(m_ref.shape, NEG_BIG, F32)
        l_ref[...] = jnp.zeros(l_ref.shape, F32)
        acc_ref[...] = jnp.zeros(acc_ref.shape, F32)

    def step(head_k, head_v, width, masked):
        if masked:
            row = lax.broadcasted_iota(jnp.int32, (2 * tq, width), 0)
            col = lax.broadcasted_iota(jnp.int32, (2 * tq, width), 1)
            visible = ((past_len + col) >> 6) <= ((past_len + (row & (tq - 1))) >> 6)
        for h in range(ATT_HEADS):
            s = lax.dot_general(qm_ref[h], head_k(h), (((1,), (1,)), ((), ())),
                                preferred_element_type=F32)
            if masked:
                s = jnp.where(visible, s, NEG_BIG)
            m_prev = m_ref[h]
            m_new = jnp.maximum(m_prev, jnp.max(s, axis=-1, keepdims=True))
            alpha = jnp.exp2(m_prev - m_new)
            p = jnp.exp2(s - jnp.concatenate([m_new] * (width // LANES), axis=1))
            l_ref[h] = alpha * l_ref[h] + jnp.sum(p, axis=-1, keepdims=True)
            acc_ref[h] = acc_ref[h] * alpha + jnp.dot(p.astype(BF16), head_v(h), preferred_element_type=F32)
            m_ref[h] = m_new

    @pl.when(j < n_past)
    def _():
        step(lambda h: jnp.concatenate([kp_ref[0, :, 2 * h, :], kp_ref[0, :, 2 * h + 1, :]],
                                       axis=1).astype(BF16),
             lambda h: vp_ref[0, :, h, :].astype(BF16), kp_ref.shape[1], False)

    @pl.when(j == n_past)
    def _():
        step(lambda h: kn_ref[0, :, h * LANES:(h + 1) * LANES],
             lambda h: vn_ref[0, :, h * LANES:(h + 1) * LANES], tn, True)
        lam = lam_ref[0, 0]
        for h in range(ATT_HEADS):
            hs = slice(h * LANES, (h + 1) * LANES)
            o = acc_ref[h] / l_ref[h]
            o = o[0:tq] - lam * o[tq:2 * tq]
            on = o * lax.rsqrt(jnp.mean(o * o, axis=-1, keepdims=True) + NORM_EPS) * sw_ref[...]
            o_ref[0, :, hs] = (on * out_scale).astype(BF16)


def _attention(lam, q, k_past, v_past, k_new, v_new, subln_w, out_scale):
    b, t, _ = q.shape
    past_len = k_past.shape[1]
    tk = min(past_len, 512)
    tn = -(-t // LANES) * LANES
    assert (t & (t - 1) == 0 and t % CHUNK == 0 and t <= 256 and past_len > 0 and past_len % tk == 0
            and past_len % CHUNK == 0 and tk % LANES == 0 and CHUNK == 64)
    n_past = past_len // tk
    if tn != t:
        k_new = jnp.pad(k_new, ((0, 0), (0, tn - t), (0, 0)))
        v_new = jnp.pad(v_new, ((0, 0), (0, tn - t), (0, 0)))
    past = lambda a: pl.BlockSpec((1, tk) + a.shape[2:], lambda bi, j: (bi, jnp.minimum(j, n_past - 1), 0, 0))
    new = pl.BlockSpec((1, tn, ATT_WIDTH), lambda bi, j: (bi, 0, 0))
    return pl.pallas_call(
        functools.partial(_attn_kernel, tq=t, tn=tn, past_len=past_len, out_scale=out_scale),
        grid=(b, n_past + 1),
        in_specs=[
            pl.BlockSpec(memory_space=pltpu.SMEM),
            pl.BlockSpec((1, t, ATT_WIDTH), lambda bi, j: (bi, 0, 0)),
            past(k_past), past(v_past), new, new,
            pl.BlockSpec((1, LANES), lambda bi, j: (0, 0)),
        ],
        out_specs=pl.BlockSpec((1, t, ATT_WIDTH), lambda bi, j: (bi, 0, 0)),
        out_shape=jax.ShapeDtypeStruct((b, t, ATT_WIDTH), BF16),
        scratch_shapes=[
            pltpu.VMEM((ATT_HEADS, 2 * t, LANES), BF16),
            pltpu.VMEM((ATT_HEADS, 2 * t, LANES), F32),
            pltpu.VMEM((ATT_HEADS, 2 * t, LANES), F32),
            pltpu.VMEM((ATT_HEADS, 2 * t, LANES), F32),
        ],
        compiler_params=_cparams(("parallel", "arbitrary")),
        name="attention",
    )(lam, q, k_past, v_past, k_new, v_new, subln_w)


def _attn_t_kernel(qblk_ref, kblk_ref, lam_ref, q_ref, k_ref, vt_ref, swb_ref, o_ref,
                   qm_ref, m_ref, l_ref, acc_ref, s_ref, bmax_ref, *, tq, tk, past_len, out_scale):
    step = pl.program_id(1)
    i = qblk_ref[step]
    j = kblk_ref[step]
    q0 = past_len + i * tq
    j_last = (((q0 + tq - 1) // CHUNK) * CHUNK + CHUNK - 1) // tk
    col_tile = 2 * LANES

    @pl.when(j == 0)
    def _():
        lo_half = lax.broadcasted_iota(jnp.int32, (tq, LANES), 1) < ATT_HEAD_DIM
        for h in range(ATT_HEADS):
            qh = q_ref[0, :, h * LANES:(h + 1) * LANES]
            zero = jnp.zeros_like(qh)
            qm_ref[h, 0:tq, :] = jnp.where(lo_half, qh, zero)
            qm_ref[h, tq:2 * tq, :] = jnp.where(lo_half, zero, qh)
        m_ref[...] = jnp.full(m_ref.shape, NEG_BIG, F32)
        l_ref[...] = jnp.zeros(l_ref.shape, F32)
        acc_ref[...] = jnp.zeros(acc_ref.shape, F32)

    def block(masked):
        if masked:
            k_chunk = (j * tk + lax.broadcasted_iota(jnp.int32, (tk, 2 * tq), 0)) >> 6
            q_col = lax.broadcasted_iota(jnp.int32, (tk, 2 * tq), 1) & (tq - 1)
            visible = k_chunk <= ((q0 + q_col) >> 6)

        def scores(h, slot):
            st = lax.dot_general(k_ref[0, h], qm_ref[h], (((1,), (1,)), ((), ())),
                                 preferred_element_type=F32)
            if masked:
                st = jnp.where(visible, st, NEG_BIG)
            s_ref[slot] = st
            bmax_ref[slot] = jnp.max(st, axis=0, keepdims=True)

        def update(h, slot):
            m_prev = m_ref[h]
            m_new = jnp.maximum(m_prev, bmax_ref[slot])
            alpha = jnp.exp2(m_prev - m_new)
            m_ref[h] = m_new
            sums, pvs = [], []
            for c in range(2 * tq // col_tile):
                cs = slice(c * col_tile, (c + 1) * col_tile)
                p = jnp.exp2(s_ref[slot, :, cs] - m_new[:, cs])
                sums.append(jnp.sum(p, axis=0, keepdims=True))
                pvs.append(jnp.dot(vt_ref[0, h], p.astype(BF16), preferred_element_type=F32))
            l_ref[h] = alpha * l_ref[h] + jnp.concatenate(sums, axis=1)
            acc_ref[h] = acc_ref[h] * alpha + jnp.concatenate(pvs, axis=1)

        scores(0, 0)

        def body(g, carry):
            h = 2 * g
            scores(h + 1, 1)
            update(h, 0)
            scores(h + 2, 0)
            update(h + 1, 1)
            return carry

        lax.fori_loop(0, ATT_HEADS // 2 - 1, body, 0)
        scores(ATT_HEADS - 1, 1)
        update(ATT_HEADS - 2, 0)
        update(ATT_HEADS - 1, 1)

    needs_mask = (j * tk + tk - 1) // CHUNK > q0 // CHUNK
    pl.when(needs_mask)(functools.partial(block, True))
    pl.when(jnp.logical_not(needs_mask))(functools.partial(block, False))

    @pl.when(j == j_last)
    def _():
        lam = lam_ref[0, 0]
        swb = jnp.concatenate([swb_ref[...]] * (tq // LANES), axis=1)
        for h in range(ATT_HEADS):
            o = acc_ref[h] / l_ref[h]
            o = o[:, 0:tq] - lam * o[:, tq:2 * tq]
            on =o * lax.rsqrt(jnp.mean(o * o, axis=0, keepdims=True) + NORM_EPS) * swb
            o_ref[0, :, h * LANES:(h + 1) * LANES] = (on * out_scale).T.astype(BF16)


def _attention_t(lam, q, k, vt, subln_w, past_len, out_scale):
    b, t, _ = q.shape
    s_len = k.shape[2]
    tq = min(t, 512)
    tk = min(s_len, 512)
    assert tq & (tq - 1) == 0 and tq >= 2 * LANES and s_len % tk == 0 and t % tq == 0 and tk % LANES == 0
    nq = t // tq
    pairs = [(i, j) for i in range(nq)
             for j in range((((past_len + i * tq + tq - 1) // CHUNK) * CHUNK + CHUNK - 1) // tk + 1)]
    qblk = jnp.asarray([p[0] for p in pairs], jnp.int32)
    kblk = jnp.asarray([p[1] for p in pairs], jnp.int32)

    swb = jnp.broadcast_to(subln_w.reshape(LANES, 1), (LANES, LANES))
    grid_spec = pltpu.PrefetchScalarGridSpec(
        num_scalar_prefetch=2,
        grid=(b, len(pairs)),
        in_specs=[
            pl.BlockSpec(memory_space=pltpu.SMEM),
            pl.BlockSpec((1, tq, ATT_WIDTH), lambda bi, s, qb, kb: (bi, qb[s], 0)),
            pl.BlockSpec((1, ATT_HEADS, tk, LANES), lambda bi, s, qb, kb: (bi, 0, kb[s], 0)),
            pl.BlockSpec((1, ATT_HEADS, LANES, tk), lambda bi, s, qb, kb: (bi, 0, 0, kb[s])),
            pl.BlockSpec((LANES, LANES), lambda bi, s, qb, kb: (0, 0)),
        ],
        out_specs=pl.BlockSpec((1, tq, ATT_WIDTH), lambda bi, s, qb, kb: (bi, qb[s], 0)),
        scratch_shapes=[
            pltpu.VMEM((ATT_HEADS, 2 * tq, LANES), BF16),
            pltpu.VMEM((ATT_HEADS, 1, 2 * tq), F32),
            pltpu.VMEM((ATT_HEADS, 1, 2 * tq), F32),
            pltpu.VMEM((ATT_HEADS, LANES, 2 * tq), F32),
            pltpu.VMEM((2, tk, 2 * tq), F32),
            pltpu.VMEM((2, 1, 2 * tq), F32),
        ],
    )
    return pl.pallas_call(
        functools.partial(_attn_t_kernel, tq=tq, tk=tk, past_len=past_len, out_scale=out_scale),
        grid_spec=grid_spec,
        out_shape=jax.ShapeDtypeStruct((b, t, ATT_WIDTH), BF16),
        compiler_params=_cparams(("parallel", "arbitrary")),
        name="attention_t",
    )(qblk, kblk, lam, q, k, vt, swb)


def _ssd_kernel(xbc_ref, z_ref, dt_ref, cw_ref, cb_ref, dtb_ref, alog_ref, dexp_ref, nw_ref, tril_ref,
                cpast_ref, spast_ref, y_ref, st_ref, xpad_ref, xcv_ref, state_ref, *, tt):
    ti = pl.program_id(1)
    L = CHUNK
    hist = SSM_CONV - 1
    base = 8
    assert SSM_CONV == 4

    @pl.when(ti == 0)
    def _():
        xpad_ref[0:base, :] = jnp.zeros((base, SSM_CONV_DIM), F32)
        xpad_ref[base - hist:base, :] = cpast_ref[0]
        state_ref[...] = spast_ref[0]

    xpad_ref[base:base + tt, :] = xbc_ref[0]
    w0, w1, w2, w3 = (cw_ref[d:d + 1, :] for d in range(SSM_CONV))
    for c in range(tt // L):
        r = base + c * L
        xa = xpad_ref[r - 8:r + L, :]
        x1 = pltpu.roll(xa, 1, 0)
        v2 = pltpu.roll(w1 * xa + w0 * x1, 2, 0)
        acc = cb_ref[...] + w3 * xa[8:] + w2 * x1[8:] + v2[8:]
        xcv_ref[c * L:(c + 1) * L, :] = _silu(acc)
    xpad_ref[base - hist:base, :] = xpad_ref[base + tt - hist:base + tt, :]

    lane = lax.broadcasted_iota(jnp.int32, (L, LANES), 1)
    lo_half = lane < SSM_HEAD_DIM
    tril2 = (lane & (SSM_HEAD_DIM - 1)) <= lax.broadcasted_iota(jnp.int32, (L, LANES), 0)
    a_neg = -jnp.exp(alog_ref[...])
    tril = tril_ref[...]
    b_off = SSM_D_INNER
    c_off = SSM_D_INNER + SSM_GROUPS * SSM_D_STATE

    def chunk(c, carry):
        r0 = pl.multiple_of(c * L, L)
        rows = pl.ds(r0, L)
        dtv = _softplus(dt_ref[0, rows, :] + dtb_ref[...])
        da = dtv * a_neg
        hi = da.astype(BF16)
        r1 = da - hi.astype(F32)
        mid = r1.astype(BF16)
        lo = (r1 - mid.astype(F32)).astype(BF16)
        acum = (jnp.dot(tril, hi, preferred_element_type=F32)
                + jnp.dot(tril, mid, preferred_element_type=F32)
                + jnp.dot(tril, lo, preferred_element_type=F32))
        acum_t = jnp.concatenate([acum, acum], axis=0).T
        for g in range(SSM_GROUPS):
            gs = slice(g * SSM_GROUP_WIDTH, (g + 1) * SSM_GROUP_WIDTH)
            bm = xcv_ref[rows, b_off + g * SSM_D_STATE:b_off + (g + 1) * SSM_D_STATE].astype(BF16)
            cm = xcv_ref[rows, c_off + g * SSM_D_STATE:c_off + (g + 1) * SSM_D_STATE].astype(BF16)
            cb2 = lax.dot_general(cm, jnp.concatenate([bm, bm], axis=0), (((1,), (1,)), ((), ())),
                                  preferred_element_type=F32)
            st = state_ref[g]
            y_in = jnp.dot(cm, st.astype(BF16), preferred_element_type=F32)
            ys, xws, lasts = [], [], []
            for pp in range(SSM_PAIRS_PER_GROUP):
                pair = g * SSM_PAIRS_PER_GROUP + pp
                h1, h2 = 2 * pair, 2 * pair + 1
                ps = slice(pair * LANES, (pair + 1) * LANES)
                a_col = jnp.where(lo_half, acum[:, h1:h1 + 1], acum[:, h2:h2 + 1])
                dt_col = jnp.where(lo_half, dtv[:, h1:h1 + 1], dtv[:, h2:h2 + 1])
                a_row = jnp.where(lo_half[0:1], acum_t[h1:h1 + 1, :], acum_t[h2:h2 + 1, :])
                last = a_col[L - 1:L, :]
                xs = xcv_ref[rows, ps]
                xdt = xs * dt_col
                w = (cb2 * jnp.exp(jnp.where(tril2, a_col - a_row, NEG_BIG))).astype(BF16)
                zero = jnp.zeros_like(xdt)
                rhs = jnp.concatenate([jnp.where(lo_half, xdt, zero), jnp.where(lo_half, zero, xdt)],
                                      axis=0).astype(BF16)
                y = (jnp.dot(w, rhs, preferred_element_type=F32)
                     + y_in[:, pp * LANES:(pp + 1) * LANES] * jnp.exp(a_col)
                     + dexp_ref[:, ps] * xs)
                ys.append(y)
                xws.append(xdt * jnp.exp(last - a_col))
                lasts.append(last)
            yg = jnp.concatenate(ys, axis=1)
            xwg = jnp.concatenate(xws, axis=1).astype(BF16)
            lastg = jnp.concatenate(lasts, axis=1)
            state_ref[g] = st * jnp.exp(lastg) + lax.dot_general(
                bm, xwg, (((0,), (0,)), ((), ())), preferred_element_type=F32)
            yg = yg * _silu(z_ref[0, rows, gs])
            yn = yg * lax.rsqrt(jnp.mean(yg * yg, axis=-1, keepdims=True) + NORM_EPS) * nw_ref[:, gs]
            y_ref[0, rows, gs] = yn.astype(BF16)
        return carry

    lax.fori_loop(0, tt // L, chunk, 0, unroll=True)

    @pl.when(ti == pl.num_programs(1) - 1)
    def _():
        st_ref[0] = state_ref[...]


def _ssd(proj3, dt3, conv_past, state_past_t, cw, cb, dtb, alog, dexp, nw, tril):
    b, t, _ = proj3.shape
    tt = min(t, 256)
    full = lambda shape: pl.BlockSpec(shape, lambda bi, i: (0,) * len(shape))
    return pl.pallas_call(
        functools.partial(_ssd_kernel, tt=tt),
        grid=(b, t // tt),
        in_specs=[
            pl.BlockSpec((1, tt, SSM_CONV_DIM), lambda bi, i: (bi, i, COL_XBC // SSM_CONV_DIM)),
            pl.BlockSpec((1, tt, SSM_D_INNER), lambda bi, i: (bi, i, COL_Z // SSM_D_INNER)),
            pl.BlockSpec((1, tt, LANES), lambda bi, i: (bi, i, 0)),
            full((SSM_CONV, SSM_CONV_DIM)), full((1, SSM_CONV_DIM)), full((1, LANES)), full((1, LANES)),
            full((1, SSM_D_INNER)), full((1, SSM_D_INNER)), full((CHUNK, CHUNK)),
            pl.BlockSpec((1, SSM_CONV - 1, SSM_CONV_DIM), lambda bi, i: (bi, 0, 0)),
            pl.BlockSpec((1, SSM_GROUPS, SSM_D_STATE, SSM_GROUP_WIDTH), lambda bi, i: (bi, 0, 0, 0)),
        ],
        out_specs=[
            pl.BlockSpec((1, tt, SSM_D_INNER), lambda bi, i: (bi, i, 0)),
            pl.BlockSpec((1, SSM_GROUPS, SSM_D_STATE, SSM_GROUP_WIDTH), lambda bi, i: (bi, 0, 0, 0)),
        ],
        out_shape=[
            jax.ShapeDtypeStruct((b, t, SSM_D_INNER), BF16),
            jax.ShapeDtypeStruct((b, SSM_GROUPS, SSM_D_STATE, SSM_GROUP_WIDTH), F32),
        ],
        scratch_shapes=[
            pltpu.VMEM((tt + 8, SSM_CONV_DIM), F32),
            pltpu.VMEM((tt, SSM_CONV_DIM), F32),
            pltpu.VMEM((SSM_GROUPS, SSM_D_STATE, SSM_GROUP_WIDTH), F32),
        ],
        compiler_params=_cparams(("parallel", "arbitrary")),
        name="ssd",
    )(proj3, proj3, dt3, cw, cb, dtb, alog, dexp, nw, tril, conv_past, state_past_t)


def _merge_kernel(x_ref, ys_ref, o_ref, gs_ref, ga_ref, wbs_ref, wba_ref, wo_ref, nfw_ref, x1_ref, hn_ref):
    bs = jnp.dot(ys_ref[...], wbs_ref[...], preferred_element_type=F32)
    ba = jnp.dot(o_ref[...], wba_ref[...], preferred_element_type=F32)
    mix = _sigmoid(gs_ref[...]) * bs + _sigmoid(ga_ref[...]) * ba
    x1 = x_ref[...] + jnp.dot(mix.astype(BF16), wo_ref[...], preferred_element_type=F32)
    x1_ref[...] = x1
    hn = x1 * lax.rsqrt(jnp.mean(x1 * x1, axis=-1, keepdims=True) + NORM_EPS) * nfw_ref[...]
    hn_ref[...] = hn.astype(BF16)


def _merge(x2d, ys, o, proj, wbs, wba, wo, nfw):
    n = x2d.shape[0]
    tm = min(n, 512)
    row = lambda w, c=0: pl.BlockSpec((tm, w), lambda i: (i, c))
    full = lambda shape: pl.BlockSpec(shape, lambda i: (0, 0))
    return pl.pallas_call(
        _merge_kernel,
        grid=(n // tm,),
        in_specs=[row(D_MODEL), row(SSM_D_INNER), row(ATT_WIDTH),
                  row(D_MODEL, COL_GS // D_MODEL), row(D_MODEL, COL_GA // D_MODEL),
                  full((SSM_D_INNER, D_MODEL)), full((ATT_WIDTH, D_MODEL)), full((D_MODEL, D_MODEL)),
                  full((1, D_MODEL))],
        out_specs=[row(D_MODEL), row(D_MODEL)],
        out_shape=[jax.ShapeDtypeStruct((n, D_MODEL), F32), jax.ShapeDtypeStruct((n, D_MODEL), BF16)],
        compiler_params=_cparams(("parallel",)),
        name="merge",
    )(x2d, ys, o, proj, proj, wbs, wba, wo, nfw)


FFN_SPLIT = 2


def _ffn_kernel(hn_ref, x1_ref, wa_ref, wb_ref, wd_ref, cw_ref, cb_ref, past_ref, out_ref, tail_ref,
                hbuf_ref, *, tm):
    ti = pl.program_id(1)
    hist = FFN_CONV - 1
    base = 8
    tf = D_FF // FFN_SPLIT

    @pl.when(ti == 0)
    def _():
        hbuf_ref[base - hist:base, :] = past_ref[0]

    hn = hn_ref[0]
    acc = x1_ref[0]
    for f in range(FFN_SPLIT):
        fs = slice(f * tf, (f + 1) * tf)
        ha = jnp.dot(hn, wa_ref[:, fs], preferred_element_type=F32)
        hb = jnp.dot(hn, wb_ref[:, fs], preferred_element_type=F32)
        hbuf_ref[base:base + tm, fs] = ha
        hc = cb_ref[:, fs] + cw_ref[hist:hist + 1, fs] * ha
        for d in range(1, FFN_CONV):
            hc = hc + cw_ref[hist - d:hist - d + 1, fs] * hbuf_ref[base - d:base - d + tm, fs]
        u = (_silu(hc) * hb).astype(BF16)
        acc = acc + jnp.dot(u, wd_ref[fs, :], preferred_element_type=F32)
    out_ref[0] = acc
    tail = hbuf_ref[base + tm - hist:base + tm, :]
    hbuf_ref[base - hist:base, :] = tail
    tail_ref[0] = tail


def _ffn(hn3, x13, wa, wb, wd, cw, cb, past):
    b, t, _ = hn3.shape
    tm = min(t, 512)
    const = lambda shape: pl.BlockSpec(shape, lambda bi, i: (0, 0), pipeline_mode=pl.Buffered(1))
    return pl.pallas_call(
        functools.partial(_ffn_kernel, tm=tm),
        grid=(b, t // tm),
        in_specs=[
            pl.BlockSpec((1, tm, D_MODEL), lambda bi, i: (bi, i, 0)),
            pl.BlockSpec((1, tm, D_MODEL), lambda bi, i: (bi, i, 0)),
            const((D_MODEL, D_FF)), const((D_MODEL, D_FF)), const((D_FF, D_MODEL)),
            const((FFN_CONV, D_FF)), const((1, D_FF)),
            pl.BlockSpec((1, FFN_CONV - 1, D_FF), lambda bi, i: (bi, 0, 0)),
        ],
        out_specs=[
            pl.BlockSpec((1, tm, D_MODEL), lambda bi, i: (bi, i, 0)),
            pl.BlockSpec((1, FFN_CONV - 1, D_FF), lambda bi, i: (bi, 0, 0)),
        ],
        out_shape=[
            jax.ShapeDtypeStruct((b, t, D_MODEL), F32),
            jax.ShapeDtypeStruct((b, FFN_CONV - 1, D_FF), F32),
        ],
        scratch_shapes=[pltpu.VMEM((tm + 8, D_FF), F32)],
        compiler_params=_cparams(("parallel", "arbitrary")),
        name="ffn",
    )(hn3, x13, wa, wb, wd, cw, cb, past)


def _prepare_weights(norm_mix_w, w_in, ssm_conv_w, ssm_conv_b, ssm_dt_bias, ssm_a_log, ssm_d, ssm_norm_w,
                     q_norm_w, k_norm_w, subln_w, w_branch_ssm, w_branch_attn, w_out, norm_ffn_w, w_up,
                     ffn_conv_w, ffn_conv_b, w_down):
    bounds = [0]
    for s in IN_SIZES:
        bounds.append(bounds[-1] + s)
    piece = lambda n: w_in[:, bounds[n]:bounds[n + 1]]
    z, xbc, dt, q, k, v, gs, ga = (piece(n) for n in range(8))
    pad_heads = lambda a: jnp.pad(a.reshape(1, SSM_HEADS), ((0, 0), (0, LANES - SSM_HEADS)))
    row = lambda a: a.reshape(1, -1)
    two_heads = lambda a: jnp.concatenate([a, a]).reshape(1, LANES)
    head = jnp.arange(LANES) // ATT_HEAD_DIM
    tri = jnp.arange(CHUNK)
    return dict(
        norm_mix=row(norm_mix_w),
        w_main=jnp.concatenate([z, q, k, v, gs, xbc, ga], axis=1).astype(BF16),
        w_dt=jnp.pad(dt, ((0, 0), (0, LANES - SSM_HEADS))).astype(BF16),
        conv_w=ssm_conv_w, conv_b=row(ssm_conv_b),
        dt_bias=pad_heads(ssm_dt_bias), a_log=pad_heads(ssm_a_log),
        d_exp=row(jnp.repeat(ssm_d, SSM_HEAD_DIM)), ssm_norm=row(ssm_norm_w),
        tril=(tri[:, None] >= tri[None, :]).astype(BF16),
        q_norm=two_heads(q_norm_w), k_norm=two_heads(k_norm_w),
        head_ones=(head[:, None] == head[None, :]).astype(BF16),
        subln=row(subln_w),
        w_bs=w_branch_ssm.astype(BF16), w_ba=w_branch_attn.astype(BF16), w_out=w_out.astype(BF16),
        norm_ffn=row(norm_ffn_w),
        w_up_a=w_up[:, :D_FF].astype(BF16), w_up_b=w_up[:, D_FF:].astype(BF16),
        w_down=w_down.astype(BF16), ffn_conv_w=ffn_conv_w, ffn_conv_b=row(ffn_conv_b),
    )


def _state_to_kernel_layout(s):
    b = s.shape[0]
    s = s.reshape(b, SSM_GROUPS, SSM_HEADS // SSM_GROUPS, SSM_HEAD_DIM, SSM_D_STATE)
    return jnp.transpose(s, (0, 1, 4, 2, 3)).reshape(b, SSM_GROUPS, SSM_D_STATE, SSM_GROUP_WIDTH)


def _state_from_kernel_layout(s):
    b = s.shape[0]
    s = s.reshape(b, SSM_GROUPS, SSM_D_STATE, SSM_HEADS // SSM_GROUPS, SSM_HEAD_DIM)
    return jnp.transpose(s, (0, 1, 3, 4, 2)).reshape(b, SSM_HEADS, SSM_HEAD_DIM, SSM_D_STATE)


def _layer(x, k_past, v_past, ssm_past, conv_past, ffn_past, lam, lambda_init, w):
    b, t, _ = x.shape
    past_len = k_past.shape[1]
    n = b * t
    x2d = x.reshape(n, D_MODEL)

    proj, dt = _in_proj(x2d, w["norm_mix"], w["w_main"], w["w_dt"])
    keys_on_rows = past_len == 0 and t % (4 * LANES) == 0
    qb, kf, kb, vf, vb = _qk_prep(proj, _rope_tables(t, past_len), w["q_norm"], w["k_norm"],
                                  w["head_ones"], t, keys_on_rows)
    q3 = qb.reshape(b, t, ATT_WIDTH)
    if keys_on_rows:
        o = _attention_t(lam, q3, kb, vb, w["subln"], past_len, 1.0 - lambda_init)
    else:
        o = _attention(lam, q3, k_past, v_past, kb.reshape(b, t, ATT_WIDTH), vb.reshape(b, t, ATT_WIDTH),
                       w["subln"], 1.0 - lambda_init)

    ys, state_t = _ssd(proj.reshape(b, t, PROJ_WIDTH), dt.reshape(b, t, LANES), conv_past,
                       _state_to_kernel_layout(ssm_past), w["conv_w"], w["conv_b"], w["dt_bias"],
                       w["a_log"], w["d_exp"], w["ssm_norm"], w["tril"])

    x1, hn = _merge(x2d, ys.reshape(n, SSM_D_INNER), o.reshape(n, ATT_WIDTH), proj,
                    w["w_bs"], w["w_ba"], w["w_out"], w["norm_ffn"])
    y, ffn_new = _ffn(hn.reshape(b, t, D_MODEL), x1.reshape(b, t, D_MODEL), w["w_up_a"], w["w_up_b"],
                      w["w_down"], w["ffn_conv_w"], w["ffn_conv_b"], ffn_past)

    xbc_rows = proj.reshape(b, t, PROJ_WIDTH)[:, t - (SSM_CONV - 1):, COL_XBC:COL_XBC + SSM_CONV_DIM]
    k_new = kf.reshape(b, t, 2 * ATT_HEADS, ATT_HEAD_DIM)
    v_new = vf.reshape(b, t, ATT_HEADS, 2 * ATT_HEAD_DIM)
    return y, k_new, v_new, _state_from_kernel_layout(state_t), xbc_rows, ffn_new


def kernel(x_prompt, x_sample, cache_k, cache_v, state_ssm, state_ssm_conv, state_ffn_conv, norm_mix_w, w_in, ssm_conv_w, ssm_conv_b, ssm_dt_bias, ssm_a_log, ssm_d, ssm_norm_w, q_norm_w, k_norm_w, lambda_q1, lambda_k1, lambda_q2, lambda_k2, subln_w, w_branch_ssm, w_branch_attn, w_out, norm_ffn_w, w_up, ffn_conv_w, ffn_conv_b, w_down):
    depth = w_in.shape[0]
    assert depth == 1
    bp = x_prompt.shape[0]
    dt_ = x_prompt.dtype
    layer = 0
    lambda_init = 0.8 - 0.6 * math.exp(-0.3 * layer)
    w = _prepare_weights(norm_mix_w[layer], w_in[layer], ssm_conv_w[layer], ssm_conv_b[layer],
                         ssm_dt_bias[layer], ssm_a_log[layer], ssm_d[layer], ssm_norm_w[layer],
                         q_norm_w[layer], k_norm_w[layer], subln_w[layer], w_branch_ssm[layer],
                         w_branch_attn[layer], w_out[layer], norm_ffn_w[layer], w_up[layer],
                         ffn_conv_w[layer], ffn_conv_b[layer], w_down[layer])
    lam = _lambda(lambda_q1[layer].reshape(1, -1), lambda_k1[layer].reshape(1, -1),
                  lambda_q2[layer].reshape(1, -1), lambda_k2[layer].reshape(1, -1), lambda_init)

    yp, kp, vp, sp, cp, fp = _layer(
        x_prompt,
        jnp.zeros((bp, 0, 2 * ATT_HEADS, ATT_HEAD_DIM), dt_),
        jnp.zeros((bp, 0, ATT_HEADS, 2 * ATT_HEAD_DIM), dt_),
        jnp.zeros((bp, SSM_HEADS, SSM_HEAD_DIM, SSM_D_STATE), dt_),
        jnp.zeros((bp, SSM_CONV - 1, SSM_CONV_DIM), dt_),
        jnp.zeros((bp, FFN_CONV - 1, D_FF), dt_),
        lam, lambda_init, w)
    ys, ks, vs, ss, cs, fs = _layer(
        x_sample, cache_k[layer], cache_v[layer], state_ssm[layer], state_ssm_conv[layer],
        state_ffn_conv[layer], lam, lambda_init, w)
    stack = lambda a: a[None]
    return (yp, ys, stack(kp), stack(vp), stack(sp), stack(cp), stack(fp),
            stack(ks), stack(vs), stack(ss), stack(cs), stack(fs))
```

```python
import functools
import math

import jax
import jax.numpy as jnp
from jax import lax
from jax.experimental import pallas as pl
from jax.experimental.pallas import tpu as pltpu

F32 = jnp.float32
BF16 = jnp.bfloat16

D_MODEL = 1024
CHUNK = 64
NORM_EPS = 1e-6

SSM_D_INNER = 2048
SSM_HEAD_DIM = 64
SSM_HEADS = 32
SSM_GROUPS = 4
SSM_D_STATE = 128
SSM_CONV = 4
SSM_CONV_DIM = 3072
SSM_GROUP_WIDTH = SSM_D_INNER // SSM_GROUPS
SSM_PAIRS_PER_GROUP = SSM_GROUP_WIDTH // 128

ATT_HEADS = 8
ATT_HEAD_DIM = 64
ATT_WIDTH = 1024
ROPE_THETA = 500000.0
ROT_DIM = 16

D_FF = 2816
FFN_CONV = 3

IN_SIZES = (SSM_D_INNER, SSM_CONV_DIM, SSM_HEADS, ATT_WIDTH, ATT_WIDTH, ATT_WIDTH, D_MODEL, D_MODEL)

PROJ_WIDTH = 10240
COL_Z, COL_Q, COL_K, COL_V, COL_GS, COL_XBC, COL_GA = 0, 2048, 3072, 4096, 5120, 6144, 9216

LANES = 128
NEG_BIG = -1e30
VMEM_LIMIT_MB = 56


def _cparams(semantics):
    return pltpu.CompilerParams(dimension_semantics=semantics,
                                vmem_limit_bytes=VMEM_LIMIT_MB * 1024 * 1024)


def _sigmoid(x):
    return 1.0 / (1.0 + jnp.exp(-x))


def _silu(x):
    h = 0.5 * x
    return h + h * jnp.tanh(h)


def _softplus(x):
    return jnp.maximum(x, 0.0) + jnp.log(1.0 + jnp.exp(-jnp.abs(x)))


def _inproj_kernel(x_ref, nw_ref, w_ref, wdt_ref, proj_ref, dt_ref, xn_ref):
    @pl.when(pl.program_id(1) == 0)
    def _():
        x = x_ref[...]
        xn = x * lax.rsqrt(jnp.mean(x * x, axis=-1, keepdims=True) + NORM_EPS) * nw_ref[...]
        xn_ref[...] = xn.astype(BF16)
        dt_ref[...] = jnp.dot(xn_ref[...], wdt_ref[...], preferred_element_type=F32)

    proj_ref[...] = jnp.dot(xn_ref[...], w_ref[...], preferred_element_type=F32)


def _in_proj(x2d, norm_w, w_main, w_dt):
    n = x2d.shape[0]
    tm = min(n, 1024)
    tn = 2048
    return pl.pallas_call(
        _inproj_kernel,
        grid=(n // tm, PROJ_WIDTH // tn),
        in_specs=[
            pl.BlockSpec((tm, D_MODEL), lambda i, j: (i, 0)),
            pl.BlockSpec((1, D_MODEL), lambda i, j: (0, 0)),
            pl.BlockSpec((D_MODEL, tn), lambda i, j: (0, j)),
            pl.BlockSpec((D_MODEL, LANES), lambda i, j: (0, 0)),
        ],
        out_specs=[
            pl.BlockSpec((tm, tn), lambda i, j: (i, j)),
            pl.BlockSpec((tm, LANES), lambda i, j: (i, 0)),
        ],
        out_shape=[
            jax.ShapeDtypeStruct((n, PROJ_WIDTH), F32),
            jax.ShapeDtypeStruct((n, LANES), F32),
        ],
        scratch_shapes=[pltpu.VMEM((tm, D_MODEL), BF16)],
        compiler_params=_cparams(("parallel", "arbitrary")),
        name="in_proj",
    )(x2d, norm_w, w_main, w_dt)


def _qkprep_kernel(q_ref, k_ref, v_ref, cos_ref, sn_ref, sp_ref, qw_ref, kw_ref, bd_ref,
                   qb_ref, kf_ref, kb_ref, vf_ref, vb_ref, *, q_scale, head_major):
    tm = q_ref.shape[0]
    bd = bd_ref[...]
    cos, sn, sp = cos_ref[...], sn_ref[...], sp_ref[...]

    def norm_rope(x, w):
        x2 = x * x
        hi = x2.astype(BF16)
        lo = (x2 - hi.astype(F32)).astype(BF16)
        ss = (jnp.dot(hi, bd, preferred_element_type=F32)
              + jnp.dot(lo, bd, preferred_element_type=F32))
        y = x * lax.rsqrt(ss * (1.0 / ATT_HEAD_DIM) + NORM_EPS) * w
        return (y * cos + pltpu.roll(y, LANES - ROT_DIM // 2, 1) * sn
                + pltpu.roll(y, ROT_DIM // 2, 1) * sp)

    for g in range(ATT_WIDTH // LANES):
        sl = slice(g * LANES, (g + 1) * LANES)
        qr = norm_rope(q_ref[:, sl], qw_ref[...])
        qb_ref[:, sl] = (qr * q_scale).astype(BF16)
        kr = norm_rope(k_ref[:, sl], kw_ref[...])
        kr_t = kr.T
        kf_ref[0, 2 * g] = kr_t[:ATT_HEAD_DIM]
        kf_ref[0, 2 * g + 1] = kr_t[ATT_HEAD_DIM:]
        v = v_ref[:, sl]
        vf_ref[pl.ds(g, tm, stride=ATT_HEADS), :] = v
        if head_major:
            kb_ref[0, g] = kr.astype(BF16)
            vb_ref[0, g] = v.T.astype(BF16)
        else:
            kb_ref[:, sl] = kr.astype(BF16)
            vb_ref[:, sl] = v.astype(BF16)


def _qk_prep(proj, tables, qw, kw, bd, t, head_major):
    n = proj.shape[0]
    tm = min(t, 512)
    nt = t // tm
    cos, sn, sp = tables
    wide = lambda c: pl.BlockSpec((tm, ATT_WIDTH), lambda i: (i, c // ATT_WIDTH))
    tab = pl.BlockSpec((tm, LANES), lambda i: (i % nt, 0))
    vec = pl.BlockSpec((1, LANES), lambda i: (0, 0))
    out = pl.BlockSpec((tm, ATT_WIDTH), lambda i: (i, 0))
    if head_major:
        kb_spec = pl.BlockSpec((1, ATT_HEADS, tm, LANES), lambda i: (i // nt, 0, i % nt, 0))
        kb_shape = jax.ShapeDtypeStruct((n // t, ATT_HEADS, t, LANES), BF16)
        vb_spec = pl.BlockSpec((1, ATT_HEADS, LANES, tm), lambda i: (i // nt, 0, 0, i % nt))
        vb_shape = jax.ShapeDtypeStruct((n // t, ATT_HEADS, LANES, t), BF16)
    else:
        kb_spec = vb_spec = out
        kb_shape = vb_shape = jax.ShapeDtypeStruct((n, ATT_WIDTH), BF16)
    q_scale = ATT_HEAD_DIM ** -0.5 * math.log2(math.e)
    return pl.pallas_call(
        functools.partial(_qkprep_kernel, q_scale=q_scale, head_major=head_major),
        grid=(n // tm,),
        in_specs=[wide(COL_Q), wide(COL_K), wide(COL_V), tab, tab, tab, vec, vec,
                  pl.BlockSpec((LANES, LANES), lambda i: (0, 0))],
        out_specs=[out,
                   pl.BlockSpec((1, 2 * ATT_HEADS, ATT_HEAD_DIM, tm), lambda i: (i // nt, 0, 0, i % nt)),
                   kb_spec,
                   pl.BlockSpec((tm * ATT_HEADS, LANES), lambda i: (i, 0)), vb_spec],
        out_shape=[
            jax.ShapeDtypeStruct((n, ATT_WIDTH), BF16),
            jax.ShapeDtypeStruct((n // t, 2 * ATT_HEADS, ATT_HEAD_DIM, t), F32),
            kb_shape,
            jax.ShapeDtypeStruct((n * ATT_HEADS, LANES), F32),
            vb_shape,
        ],
        compiler_params=_cparams(("parallel",)),
        name="qk_prep",
    )(proj, proj, proj, cos, sn, sp, qw, kw, bd)


def _rope_tables(t, past_len):
    half = ROT_DIM // 2
    inv_freq = jnp.power(ROPE_THETA, -jnp.arange(half, dtype=F32) * 2.0 / ROT_DIM)
    pos = past_len + jnp.arange(t, dtype=jnp.int32)
    ang = pos.astype(F32)[:, None] * inv_freq[None, :]
    cos, sin = jnp.cos(ang), jnp.sin(ang)
    ones = jnp.ones((t, ATT_HEAD_DIM - ROT_DIM), F32)
    zeros_h = jnp.zeros((t, half), F32)
    zeros_r = jnp.zeros((t, ATT_HEAD_DIM - ROT_DIM), F32)
    cos_h = jnp.concatenate([cos, cos, ones], axis=1)
    sn_h = jnp.concatenate([-sin, zeros_h, zeros_r], axis=1)
    sp_h = jnp.concatenate([zeros_h, sin, zeros_r], axis=1)
    two = lambda a: jnp.concatenate([a, a], axis=1)
    return two(cos_h), two(sn_h), two(sp_h)


def _lam_kernel(q1_ref, k1_ref, q2_ref, k2_ref, out_ref, *, lambda_init):
    a = jnp.sum(q1_ref[...] * k1_ref[...], axis=-1, keepdims=True)
    b = jnp.sum(q2_ref[...] * k2_ref[...], axis=-1, keepdims=True)
    out_ref[...] = jnp.exp(a) - jnp.exp(b) + lambda_init


def _lambda(q1, k1, q2, k2, lambda_init):
    return pl.pallas_call(
        functools.partial(_lam_kernel, lambda_init=lambda_init),
        out_shape=jax.ShapeDtypeStruct((1, 1), F32),
        name="lambda",
    )(q1, k1, q2, k2)


def _attn_kernel(lam_ref, q_ref, kp_ref, vp_ref, kn_ref, vn_ref, sw_ref, o_ref, qm_ref, m_ref, l_ref, acc_ref,
                 *, tq, tn, past_len, out_scale):
    j = pl.program_id(1)
    n_past = pl.num_programs(1) - 1

    @pl.when(j == 0)
    def _():
        lo_half = lax.broadcasted_iota(jnp.int32, (tq, LANES), 1) < ATT_HEAD_DIM
        for h in range(ATT_HEADS):
            qh = q_ref[0, :, h * LANES:(h + 1) * LANES]
            zero = jnp.zeros_like(qh)
            qm_ref[h, 0:tq, :] = jnp.where(lo_half, qh, zero)
            qm_ref[h, tq:2 * tq, :] = jnp.where(lo_half, zero, qh)
        m_ref[...] = jnp.full(m_ref.shape, NEG_BIG, F32)
        l_ref[...] = jnp.zeros(l_ref.shape, F32)
        acc_ref[...] = jnp.zeros(acc_ref.shape, F32)

    def step(head_k, keys_on_lanes, head_v, width, masked):
        if masked:
            row = lax.broadcasted_iota(jnp.int32, (2 * tq, width), 0)
            col = lax.broadcasted_iota(jnp.int32, (2 * tq, width), 1)
            visible = ((past_len + col) >> 6) <= ((past_len + (row & (tq - 1))) >> 6)
        for h in range(ATT_HEADS):
            s = lax.dot_general(qm_ref[h], head_k(h), (((1,), (0 if keys_on_lanes else 1,)), ((), ())),
                                preferred_element_type=F32)
            if masked:
                s = jnp.where(visible, s, NEG_BIG)
            m_prev = m_ref[h]
            m_new = jnp.maximum(m_prev, jnp.max(s, axis=-1, keepdims=True))
            alpha = jnp.exp2(m_prev - m_new)
            p = jnp.exp2(s - jnp.concatenate([m_new] * (width // LANES), axis=1))
            l_ref[h] = alpha * l_ref[h] + jnp.sum(p, axis=-1, keepdims=True)
            acc_ref[h] = acc_ref[h] * alpha + jnp.dot(p.astype(BF16), head_v(h), preferred_element_type=F32)
            m_ref[h] = m_new

    @pl.when(j < n_past)
    def _():
        step(lambda h: jnp.concatenate([kp_ref[0, 2 * h], kp_ref[0, 2 * h + 1]], axis=0).astype(BF16), True,
             lambda h: vp_ref[0, :, h, :].astype(BF16), kp_ref.shape[3], False)

    @pl.when(j == n_past)
    def _():
        step(lambda h: kn_ref[0, :, h * LANES:(h + 1) * LANES], False,
             lambda h: vn_ref[0, :, h * LANES:(h + 1) * LANES], tn, True)
        lam = lam_ref[0, 0]
        for h in range(ATT_HEADS):
            hs = slice(h * LANES, (h + 1) * LANES)
            o = acc_ref[h] / l_ref[h]
            o = o[0:tq] - lam * o[tq:2 * tq]
            on = o * lax.rsqrt(jnp.mean(o * o, axis=-1, keepdims=True) + NORM_EPS) * sw_ref[...]
            o_ref[0, :, hs] = (on * out_scale).astype(BF16)


def _attention(lam, q, k_past_t, v_past, k_new, v_new, subln_w, out_scale):
    b, t, _ = q.shape
    past_len = v_past.shape[1]
    tk = min(past_len, 512)
    tn = -(-t // LANES) * LANES
    assert (t & (t - 1) == 0 and t % CHUNK == 0 and t <= 256 and past_len > 0 and past_len % tk == 0
            and past_len % CHUNK == 0 and tk % LANES == 0 and CHUNK == 64)
    n_past = past_len // tk
    if tn != t:
        k_new = jnp.pad(k_new, ((0, 0), (0, tn - t), (0, 0)))
        v_new = jnp.pad(v_new, ((0, 0), (0, tn - t), (0, 0)))
    past_k = pl.BlockSpec((1, 2 * ATT_HEADS, ATT_HEAD_DIM, tk),
                          lambda bi, j: (bi, 0, 0, jnp.minimum(j, n_past - 1)))
    past_v = pl.BlockSpec((1, tk, ATT_HEADS, 2 * ATT_HEAD_DIM),
                          lambda bi, j: (bi, jnp.minimum(j, n_past - 1), 0, 0))
    new = pl.BlockSpec((1, tn, ATT_WIDTH), lambda bi, j: (bi, 0, 0))
    return pl.pallas_call(
        functools.partial(_attn_kernel, tq=t, tn=tn, past_len=past_len, out_scale=out_scale),
        grid=(b, n_past + 1),
        in_specs=[
            pl.BlockSpec(memory_space=pltpu.SMEM),
            pl.BlockSpec((1, t, ATT_WIDTH), lambda bi, j: (bi, 0, 0)),
            past_k, past_v, new, new,
            pl.BlockSpec((1, LANES), lambda bi, j: (0, 0)),
        ],
        out_specs=pl.BlockSpec((1, t, ATT_WIDTH), lambda bi, j: (bi, 0, 0)),
        out_shape=jax.ShapeDtypeStruct((b, t, ATT_WIDTH), BF16),
        scratch_shapes=[
            pltpu.VMEM((ATT_HEADS, 2 * t, LANES), BF16),
            pltpu.VMEM((ATT_HEADS, 2 * t, LANES), F32),
            pltpu.VMEM((ATT_HEADS, 2 * t, LANES), F32),
            pltpu.VMEM((ATT_HEADS, 2 * t, LANES), F32),
        ],
        compiler_params=_cparams(("parallel", "arbitrary")),
        name="attention",
    )(lam, q, k_past_t, v_past, k_new, v_new, subln_w)


def _attn_t_kernel(qblk_ref, kblk_ref, lam_ref, q_ref, k_ref, vt_ref, swb_ref, o_ref,
                   qm_ref, m_ref, l_ref, acc_ref, s_ref, bmax_ref, *, tq, tk, past_len, out_scale):
    step = pl.program_id(1)
    i = qblk_ref[step]
    j = kblk_ref[step]
    q0 = past_len + i * tq
    j_last = (((q0 + tq - 1) // CHUNK) * CHUNK + CHUNK - 1) // tk
    col_tile = 2 * LANES

    @pl.when(j == 0)
    def _():
        lo_half = lax.broadcasted_iota(jnp.int32, (tq, LANES), 1) < ATT_HEAD_DIM
        for h in range(ATT_HEADS):
            qh = q_ref[0, :, h * LANES:(h + 1) * LANES]
            zero = jnp.zeros_like(qh)
            qm_ref[h, 0:tq, :] = jnp.where(lo_half, qh, zero)
            qm_ref[h, tq:2 * tq, :] = jnp.where(lo_half, zero, qh)
        m_ref[...] = jnp.full(m_ref.shape, NEG_BIG, F32)
        l_ref[...] = jnp.zeros(l_ref.shape, F32)
        acc_ref[...] = jnp.zeros(acc_ref.shape, F32)

    def block(masked):
        if masked:
            k_chunk = (j * tk + lax.broadcasted_iota(jnp.int32, (tk, 2 * tq), 0)) >> 6
            q_col = lax.broadcasted_iota(jnp.int32, (tk, 2 * tq), 1) & (tq - 1)
            visible = k_chunk <= ((q0 + q_col) >> 6)

        def scores(h, slot):
            st = lax.dot_general(k_ref[0, h], qm_ref[h], (((1,), (1,)), ((), ())),
                                 preferred_element_type=F32)
            if masked:
                st = jnp.where(visible, st, NEG_BIG)
            s_ref[slot] = st
            bmax_ref[slot] = jnp.max(st, axis=0, keepdims=True)

        def update(h, slot):
            m_prev = m_ref[h]
            m_new = jnp.maximum(m_prev, bmax_ref[slot])
            alpha = jnp.exp2(m_prev - m_new)
            m_ref[h] = m_new
            sums, pvs = [], []
            for c in range(2 * tq // col_tile):
                cs = slice(c * col_tile, (c + 1) * col_tile)
                p = jnp.exp2(s_ref[slot, :, cs] - m_new[:, cs])
                sums.append(jnp.sum(p, axis=0, keepdims=True))
                pvs.append(jnp.dot(vt_ref[0, h], p.astype(BF16), preferred_element_type=F32))
            l_ref[h] = alpha * l_ref[h] + jnp.concatenate(sums, axis=1)
            acc_ref[h] = acc_ref[h] * alpha + jnp.concatenate(pvs, axis=1)

        scores(0, 0)
        for h in range(ATT_HEADS):
            if h + 1 < ATT_HEADS:
                scores(h + 1, (h + 1) % 2)
            update(h, h % 2)

    needs_mask = (j * tk + tk - 1) // CHUNK > q0 // CHUNK
    pl.when(needs_mask)(functools.partial(block, True))
    pl.when(jnp.logical_not(needs_mask))(functools.partial(block, False))

    @pl.when(j == j_last)
    def _():
        lam = lam_ref[0, 0]
        swb = jnp.concatenate([swb_ref[...]] * (tq // LANES), axis=1)
        for h in range(ATT_HEADS):
            o = acc_ref[h] / l_ref[h]
            o = o[:, 0:tq] - lam * o[:, tq:2 * tq]
            on =o * lax.rsqrt(jnp.mean(o * o, axis=0, keepdims=True) + NORM_EPS) * swb
            o_ref[0, :, h * LANES:(h + 1) * LANES] = (on * out_scale).T.astype(BF16)


def _attention_t(lam, q, k, vt, subln_w, past_len, out_scale):
    b, t, _ = q.shape
    s_len = k.shape[2]
    tq = min(t, 512)
    tk = min(s_len, 512)
    assert tq & (tq - 1) == 0 and tq >= 2 * LANES and s_len % tk == 0 and t % tq == 0 and tk % LANES == 0
    nq = t // tq
    pairs = [(i, j) for i in range(nq)
             for j in range((((past_len + i * tq + tq - 1) // CHUNK) * CHUNK + CHUNK - 1) // tk + 1)]
    qblk = jnp.asarray([p[0] for p in pairs], jnp.int32)
    kblk = jnp.asarray([p[1] for p in pairs], jnp.int32)

    swb = jnp.broadcast_to(subln_w.reshape(LANES, 1), (LANES, LANES))
    grid_spec = pltpu.PrefetchScalarGridSpec(
        num_scalar_prefetch=2,
        grid=(b, len(pairs)),
        in_specs=[
            pl.BlockSpec(memory_space=pltpu.SMEM),
            pl.BlockSpec((1, tq, ATT_WIDTH), lambda bi, s, qb, kb: (bi, qb[s], 0)),
            pl.BlockSpec((1, ATT_HEADS, tk, LANES), lambda bi, s, qb, kb: (bi, 0, kb[s], 0)),
            pl.BlockSpec((1, ATT_HEADS, LANES, tk), lambda bi, s, qb, kb: (bi, 0, 0, kb[s])),
            pl.BlockSpec((LANES, LANES), lambda bi, s, qb, kb: (0, 0)),
        ],
        out_specs=pl.BlockSpec((1, tq, ATT_WIDTH), lambda bi, s, qb, kb: (bi, qb[s], 0)),
        scratch_shapes=[
            pltpu.VMEM((ATT_HEADS, 2 * tq, LANES), BF16),
            pltpu.VMEM((ATT_HEADS, 1, 2 * tq), F32),
            pltpu.VMEM((ATT_HEADS, 1, 2 * tq), F32),
            pltpu.VMEM((ATT_HEADS, LANES, 2 * tq), F32),
            pltpu.VMEM((2, tk, 2 * tq), F32),
            pltpu.VMEM((2, 1, 2 * tq), F32),
        ],
    )
    return pl.pallas_call(
        functools.partial(_attn_t_kernel, tq=tq, tk=tk, past_len=past_len, out_scale=out_scale),
        grid_spec=grid_spec,
        out_shape=jax.ShapeDtypeStruct((b, t, ATT_WIDTH), BF16),
        compiler_params=_cparams(("parallel", "arbitrary")),
        name="attention_t",
    )(qblk, kblk, lam, q, k, vt, swb)


def _ssd_kernel(xbc_ref, z_ref, dt_ref, cw_ref, cb_ref, dtb_ref, alog_ref, dexp_ref, nw_ref, tril_ref,
                cpast_ref, spast_ref, y_ref, st_ref, xpad_ref, xcv_ref, state_ref, *, tt):
    ti = pl.program_id(1)
    L = CHUNK
    hist = SSM_CONV - 1
    base = 8
    assert SSM_CONV == 4

    @pl.when(ti == 0)
    def _():
        xpad_ref[0:base, :] = jnp.zeros((base, SSM_CONV_DIM), F32)
        xpad_ref[base - hist:base, :] = cpast_ref[0]
        state_ref[...] = spast_ref[0]

    xpad_ref[base:base + tt, :] = xbc_ref[0]
    w0, w1, w2, w3 = (cw_ref[d:d + 1, :] for d in range(SSM_CONV))
    for c in range(tt // L):
        r = base + c * L
        xa = xpad_ref[r - 8:r + L, :]
        x1 = pltpu.roll(xa, 1, 0)
        v2 = pltpu.roll(w1 * xa + w0 * x1, 2, 0)
        acc = cb_ref[...] + w3 * xa[8:] + w2 * x1[8:] + v2[8:]
        xcv_ref[c * L:(c + 1) * L, :] = _silu(acc)
    xpad_ref[base - hist:base, :] = xpad_ref[base + tt - hist:base + tt, :]

    lane = lax.broadcasted_iota(jnp.int32, (L, LANES), 1)
    lo_half = lane < SSM_HEAD_DIM
    tril2 = (lane & (SSM_HEAD_DIM - 1)) <= lax.broadcasted_iota(jnp.int32, (L, LANES), 0)
    a_neg = -jnp.exp(alog_ref[...]) * math.log2(math.e)
    tril = tril_ref[...]
    b_off = SSM_D_INNER
    c_off = SSM_D_INNER + SSM_GROUPS * SSM_D_STATE

    def chunk(c, carry):
        r0 = pl.multiple_of(c * L, L)
        rows = pl.ds(r0, L)
        dtv = _softplus(dt_ref[0, rows, :] + dtb_ref[...])
        da = dtv * a_neg
        hi = da.astype(BF16)
        r1 = da - hi.astype(F32)
        mid = r1.astype(BF16)
        lo = (r1 - mid.astype(F32)).astype(BF16)
        acum = (jnp.dot(tril, hi, preferred_element_type=F32)
                + jnp.dot(tril, mid, preferred_element_type=F32)
                + jnp.dot(tril, lo, preferred_element_type=F32))
        acum_t = jnp.concatenate([acum, acum], axis=0).T
        for g in range(SSM_GROUPS):
            gs = slice(g * SSM_GROUP_WIDTH, (g + 1) * SSM_GROUP_WIDTH)
            bm = xcv_ref[rows, b_off + g * SSM_D_STATE:b_off + (g + 1) * SSM_D_STATE].astype(BF16)
            cm = xcv_ref[rows, c_off + g * SSM_D_STATE:c_off + (g + 1) * SSM_D_STATE].astype(BF16)
            cb2 = lax.dot_general(cm, jnp.concatenate([bm, bm], axis=0), (((1,), (1,)), ((), ())),
                                  preferred_element_type=F32)
            st = state_ref[g]
            y_in = jnp.dot(cm, st.astype(BF16), preferred_element_type=F32)
            ys, xws, lasts = [], [], []
            for pp in range(SSM_PAIRS_PER_GROUP):
                pair = g * SSM_PAIRS_PER_GROUP + pp
                h1, h2 = 2 * pair, 2 * pair + 1
                ps = slice(pair * LANES, (pair + 1) * LANES)
                a_col = jnp.where(lo_half, acum[:, h1:h1 + 1], acum[:, h2:h2 + 1])
                dt_col = jnp.where(lo_half, dtv[:, h1:h1 + 1], dtv[:, h2:h2 + 1])
                a_row = jnp.where(lo_half[0:1], acum_t[h1:h1 + 1, :], acum_t[h2:h2 + 1, :])
                last = a_col[L - 1:L, :]
                xs = xcv_ref[rows, ps]
                xdt = xs * dt_col
                w = (cb2 * jnp.exp2(jnp.where(tril2, a_col - a_row, NEG_BIG))).astype(BF16)
                zero = jnp.zeros_like(xdt)
                rhs = jnp.concatenate([jnp.where(lo_half, xdt, zero), jnp.where(lo_half, zero, xdt)],
                                      axis=0).astype(BF16)
                y = (jnp.dot(w, rhs, preferred_element_type=F32)
                     + y_in[:, pp * LANES:(pp + 1) * LANES] * jnp.exp2(a_col)
                     + dexp_ref[:, ps] * xs)
                ys.append(y)
                xws.append(xdt * jnp.exp2(last - a_col))
                lasts.append(last)
            yg = jnp.concatenate(ys, axis=1)
            xwg = jnp.concatenate(xws, axis=1).astype(BF16)
            lastg = jnp.concatenate(lasts, axis=1)
            state_ref[g] = st * jnp.exp2(lastg) + lax.dot_general(
                bm, xwg, (((0,), (0,)), ((), ())), preferred_element_type=F32)
            yg = yg * _silu(z_ref[0, rows, gs])
            yn = yg * lax.rsqrt(jnp.mean(yg * yg, axis=-1, keepdims=True) + NORM_EPS) * nw_ref[:, gs]
            y_ref[0, rows, gs] = yn.astype(BF16)
        return carry

    lax.fori_loop(0, tt // L, chunk, 0, unroll=True)

    @pl.when(ti == pl.num_programs(1) - 1)
    def _():
        st_ref[0] = state_ref[...]


def _ssd(proj3, dt3, conv_past, state_past_t, cw, cb, dtb, alog, dexp, nw, tril):
    b, t, _ = proj3.shape
    tt = min(t, 256)
    full = lambda shape: pl.BlockSpec(shape, lambda bi, i: (0,) * len(shape))
    return pl.pallas_call(
        functools.partial(_ssd_kernel, tt=tt),
        grid=(b, t // tt),
        in_specs=[
            pl.BlockSpec((1, tt, SSM_CONV_DIM), lambda bi, i: (bi, i, COL_XBC // SSM_CONV_DIM)),
            pl.BlockSpec((1, tt, SSM_D_INNER), lambda bi, i: (bi, i, COL_Z // SSM_D_INNER)),
            pl.BlockSpec((1, tt, LANES), lambda bi, i: (bi, i, 0)),
            full((SSM_CONV, SSM_CONV_DIM)), full((1, SSM_CONV_DIM)), full((1, LANES)), full((1, LANES)),
            full((1, SSM_D_INNER)), full((1, SSM_D_INNER)), full((CHUNK, CHUNK)),
            pl.BlockSpec((1, SSM_CONV - 1, SSM_CONV_DIM), lambda bi, i: (bi, 0, 0)),
            pl.BlockSpec((1, SSM_GROUPS, SSM_D_STATE, SSM_GROUP_WIDTH), lambda bi, i: (bi, 0, 0, 0)),
        ],
        out_specs=[
            pl.BlockSpec((1, tt, SSM_D_INNER), lambda bi, i: (bi, i, 0)),
            pl.BlockSpec((1, SSM_GROUPS, SSM_D_STATE, SSM_GROUP_WIDTH), lambda bi, i: (bi, 0, 0, 0)),
        ],
        out_shape=[
            jax.ShapeDtypeStruct((b, t, SSM_D_INNER), BF16),
            jax.ShapeDtypeStruct((b, SSM_GROUPS, SSM_D_STATE, SSM_GROUP_WIDTH), F32),
        ],
        scratch_shapes=[
            pltpu.VMEM((tt + 8, SSM_CONV_DIM), F32),
            pltpu.VMEM((tt, SSM_CONV_DIM), F32),
            pltpu.VMEM((SSM_GROUPS, SSM_D_STATE, SSM_GROUP_WIDTH), F32),
        ],
        compiler_params=_cparams(("parallel", "arbitrary")),
        name="ssd",
    )(proj3, proj3, dt3, cw, cb, dtb, alog, dexp, nw, tril, conv_past, state_past_t)


def _merge_kernel(x_ref, ys_ref, o_ref, gs_ref, ga_ref, wbs_ref, wba_ref, wo_ref, nfw_ref, x1_ref, hn_ref):
    bs = jnp.dot(ys_ref[...], wbs_ref[...], preferred_element_type=F32)
    ba = jnp.dot(o_ref[...], wba_ref[...], preferred_element_type=F32)
    mix = _sigmoid(gs_ref[...]) * bs + _sigmoid(ga_ref[...]) * ba
    x1 = x_ref[...] + jnp.dot(mix.astype(BF16), wo_ref[...], preferred_element_type=F32)
    x1_ref[...] = x1
    hn = x1 * lax.rsqrt(jnp.mean(x1 * x1, axis=-1, keepdims=True) + NORM_EPS) * nfw_ref[...]
    hn_ref[...] = hn.astype(BF16)


def _merge(x2d, ys, o, proj, wbs, wba, wo, nfw):
    n = x2d.shape[0]
    tm = min(n, 512)
    row = lambda w, c=0: pl.BlockSpec((tm, w), lambda i: (i, c))
    full = lambda shape: pl.BlockSpec(shape, lambda i: (0, 0))
    return pl.pallas_call(
        _merge_kernel,
        grid=(n // tm,),
        in_specs=[row(D_MODEL), row(SSM_D_INNER), row(ATT_WIDTH),
                  row(D_MODEL, COL_GS // D_MODEL), row(D_MODEL, COL_GA // D_MODEL),
                  full((SSM_D_INNER, D_MODEL)), full((ATT_WIDTH, D_MODEL)), full((D_MODEL, D_MODEL)),
                  full((1, D_MODEL))],
        out_specs=[row(D_MODEL), row(D_MODEL)],
        out_shape=[jax.ShapeDtypeStruct((n, D_MODEL), F32), jax.ShapeDtypeStruct((n, D_MODEL), BF16)],
        compiler_params=_cparams(("parallel",)),
        name="merge",
    )(x2d, ys, o, proj, proj, wbs, wba, wo, nfw)


FFN_SPLIT = 2


def _ffn_kernel(hn_ref, x1_ref, wa_ref, wb_ref, wd_ref, cw_ref, cb_ref, past_ref, out_ref, tail_ref,
                hbuf_ref, *, tm):
    ti = pl.program_id(1)
    hist = FFN_CONV - 1
    base = 8
    tf = D_FF // FFN_SPLIT

    @pl.when(ti == 0)
    def _():
        hbuf_ref[base - hist:base, :] = past_ref[0]

    hn = hn_ref[0]
    acc = x1_ref[0]
    for f in range(FFN_SPLIT):
        fs = slice(f * tf, (f + 1) * tf)
        ha = jnp.dot(hn, wa_ref[:, fs], preferred_element_type=F32)
        hb = jnp.dot(hn, wb_ref[:, fs], preferred_element_type=F32)
        hbuf_ref[base:base + tm, fs] = ha
        hc = cb_ref[:, fs] + cw_ref[hist:hist + 1, fs] * ha
        for d in range(1, FFN_CONV):
            hc = hc + cw_ref[hist - d:hist - d + 1, fs] * hbuf_ref[base - d:base - d + tm, fs]
        u = (_silu(hc) * hb).astype(BF16)
        acc = acc + jnp.dot(u, wd_ref[fs, :], preferred_element_type=F32)
    out_ref[0] = acc
    tail = hbuf_ref[base + tm - hist:base + tm, :]
    hbuf_ref[base - hist:base, :] = tail
    tail_ref[0] = tail


def _ffn(hn3, x13, wa, wb, wd, cw, cb, past):
    b, t, _ = hn3.shape
    tm = min(t, 512)
    const = lambda shape: pl.BlockSpec(shape, lambda bi, i: (0, 0), pipeline_mode=pl.Buffered(1))
    return pl.pallas_call(
        functools.partial(_ffn_kernel, tm=tm),
        grid=(b, t // tm),
        in_specs=[
            pl.BlockSpec((1, tm, D_MODEL), lambda bi, i: (bi, i, 0)),
            pl.BlockSpec((1, tm, D_MODEL), lambda bi, i: (bi, i, 0)),
            const((D_MODEL, D_FF)), const((D_MODEL, D_FF)), const((D_FF, D_MODEL)),
            const((FFN_CONV, D_FF)), const((1, D_FF)),
            pl.BlockSpec((1, FFN_CONV - 1, D_FF), lambda bi, i: (bi, 0, 0)),
        ],
        out_specs=[
            pl.BlockSpec((1, tm, D_MODEL), lambda bi, i: (bi, i, 0)),
            pl.BlockSpec((1, FFN_CONV - 1, D_FF), lambda bi, i: (bi, 0, 0)),
        ],
        out_shape=[
            jax.ShapeDtypeStruct((b, t, D_MODEL), F32),
            jax.ShapeDtypeStruct((b, FFN_CONV - 1, D_FF), F32),
        ],
        scratch_shapes=[pltpu.VMEM((tm + 8, D_FF), F32)],
        compiler_params=_cparams(("parallel", "arbitrary")),
        name="ffn",
    )(hn3, x13, wa, wb, wd, cw, cb, past)


def _prepare_weights(norm_mix_w, w_in, ssm_conv_w, ssm_conv_b, ssm_dt_bias, ssm_a_log, ssm_d, ssm_norm_w,
                     q_norm_w, k_norm_w, subln_w, w_branch_ssm, w_branch_attn, w_out, norm_ffn_w, w_up,
                     ffn_conv_w, ffn_conv_b, w_down):
    bounds = [0]
    for s in IN_SIZES:
        bounds.append(bounds[-1] + s)
    piece = lambda n: w_in[:, bounds[n]:bounds[n + 1]]
    z, xbc, dt, q, k, v, gs, ga = (piece(n) for n in range(8))
    pad_heads = lambda a: jnp.pad(a.reshape(1, SSM_HEADS), ((0, 0), (0, LANES - SSM_HEADS)))
    row = lambda a: a.reshape(1, -1)
    two_heads = lambda a: jnp.concatenate([a, a]).reshape(1, LANES)
    head = jnp.arange(LANES) // ATT_HEAD_DIM
    tri = jnp.arange(CHUNK)
    return dict(
        norm_mix=row(norm_mix_w),
        w_main=jnp.concatenate([z, q, k, v, gs, xbc, ga], axis=1).astype(BF16),
        w_dt=jnp.pad(dt, ((0, 0), (0, LANES - SSM_HEADS))).astype(BF16),
        conv_w=ssm_conv_w, conv_b=row(ssm_conv_b),
        dt_bias=pad_heads(ssm_dt_bias), a_log=pad_heads(ssm_a_log),
        d_exp=row(jnp.repeat(ssm_d, SSM_HEAD_DIM)), ssm_norm=row(ssm_norm_w),
        tril=(tri[:, None] >= tri[None, :]).astype(BF16),
        q_norm=two_heads(q_norm_w), k_norm=two_heads(k_norm_w),
        head_ones=(head[:, None] == head[None, :]).astype(BF16),
        subln=row(subln_w),
        w_bs=w_branch_ssm.astype(BF16), w_ba=w_branch_attn.astype(BF16), w_out=w_out.astype(BF16),
        norm_ffn=row(norm_ffn_w),
        w_up_a=w_up[:, :D_FF].astype(BF16), w_up_b=w_up[:, D_FF:].astype(BF16),
        w_down=w_down.astype(BF16), ffn_conv_w=ffn_conv_w, ffn_conv_b=row(ffn_conv_b),
    )


def _state_to_kernel_layout(s):
    b = s.shape[0]
    s = s.reshape(b, SSM_GROUPS, SSM_HEADS // SSM_GROUPS, SSM_HEAD_DIM, SSM_D_STATE)
    return jnp.transpose(s, (0, 1, 4, 2, 3)).reshape(b, SSM_GROUPS, SSM_D_STATE, SSM_GROUP_WIDTH)


def _state_from_kernel_layout(s):
    b = s.shape[0]
    s = s.reshape(b, SSM_GROUPS, SSM_D_STATE, SSM_HEADS // SSM_GROUPS, SSM_HEAD_DIM)
    return jnp.transpose(s, (0, 1, 3, 4, 2)).reshape(b, SSM_HEADS, SSM_HEAD_DIM, SSM_D_STATE)


def _layer(x, k_past, v_past, ssm_past, conv_past, ffn_past, lam, lambda_init, w):
    b, t, _ = x.shape
    past_len = k_past.shape[1]
    n = b * t
    x2d = x.reshape(n, D_MODEL)

    proj, dt = _in_proj(x2d, w["norm_mix"], w["w_main"], w["w_dt"])
    keys_on_rows = past_len == 0 and t % (4 * LANES) == 0
    qb, kf, kb, vf, vb = _qk_prep(proj, _rope_tables(t, past_len), w["q_norm"], w["k_norm"],
                                  w["head_ones"], t, keys_on_rows)
    q3 = qb.reshape(b, t, ATT_WIDTH)
    if keys_on_rows:
        o = _attention_t(lam, q3, kb, vb, w["subln"], past_len, 1.0 - lambda_init)
    else:
        o = _attention(lam, q3, jnp.transpose(k_past, (0, 2, 3, 1)), v_past, kb.reshape(b, t, ATT_WIDTH),
                       vb.reshape(b, t, ATT_WIDTH), w["subln"], 1.0 - lambda_init)

    ys, state_t = _ssd(proj.reshape(b, t, PROJ_WIDTH), dt.reshape(b, t, LANES), conv_past,
                       _state_to_kernel_layout(ssm_past), w["conv_w"], w["conv_b"], w["dt_bias"],
                       w["a_log"], w["d_exp"], w["ssm_norm"], w["tril"])

    x1, hn = _merge(x2d, ys.reshape(n, SSM_D_INNER), o.reshape(n, ATT_WIDTH), proj,
                    w["w_bs"], w["w_ba"], w["w_out"], w["norm_ffn"])
    y, ffn_new = _ffn(hn.reshape(b, t, D_MODEL), x1.reshape(b, t, D_MODEL), w["w_up_a"], w["w_up_b"],
                      w["w_down"], w["ffn_conv_w"], w["ffn_conv_b"], ffn_past)

    xbc_rows = proj.reshape(b, t, PROJ_WIDTH)[:, t - (SSM_CONV - 1):, COL_XBC:COL_XBC + SSM_CONV_DIM]
    k_new = jnp.transpose(kf, (0, 3, 1, 2))
    v_new = vf.reshape(b, t, ATT_HEADS, 2 * ATT_HEAD_DIM)
    return y, k_new, v_new, _state_from_kernel_layout(state_t), xbc_rows, ffn_new


def kernel(x_prompt, x_sample, cache_k, cache_v, state_ssm, state_ssm_conv, state_ffn_conv, norm_mix_w, w_in, ssm_conv_w, ssm_conv_b, ssm_dt_bias, ssm_a_log, ssm_d, ssm_norm_w, q_norm_w, k_norm_w, lambda_q1, lambda_k1, lambda_q2, lambda_k2, subln_w, w_branch_ssm, w_branch_attn, w_out, norm_ffn_w, w_up, ffn_conv_w, ffn_conv_b, w_down):
    depth = w_in.shape[0]
    assert depth == 1
    bp = x_prompt.shape[0]
    dt_ = x_prompt.dtype
    layer = 0
    lambda_init = 0.8 - 0.6 * math.exp(-0.3 * layer)
    w = _prepare_weights(norm_mix_w[layer], w_in[layer], ssm_conv_w[layer], ssm_conv_b[layer],
                         ssm_dt_bias[layer], ssm_a_log[layer], ssm_d[layer], ssm_norm_w[layer],
                         q_norm_w[layer], k_norm_w[layer], subln_w[layer], w_branch_ssm[layer],
                         w_branch_attn[layer], w_out[layer], norm_ffn_w[layer], w_up[layer],
                         ffn_conv_w[layer], ffn_conv_b[layer], w_down[layer])
    lam = _lambda(lambda_q1[layer].reshape(1, -1), lambda_k1[layer].reshape(1, -1),
                  lambda_q2[layer].reshape(1, -1), lambda_k2[layer].reshape(1, -1), lambda_init)

    yp, kp, vp, sp, cp, fp = _layer(
        x_prompt,
        jnp.zeros((bp, 0, 2 * ATT_HEADS, ATT_HEAD_DIM), dt_),
        jnp.zeros((bp, 0, ATT_HEADS, 2 * ATT_HEAD_DIM), dt_),
        jnp.zeros((bp, SSM_HEADS, SSM_HEAD_DIM, SSM_D_STATE), dt_),
        jnp.zeros((bp, SSM_CONV - 1, SSM_CONV_DIM), dt_),
        jnp.zeros((bp, FFN_CONV - 1, D_FF), dt_),
        lam, lambda_init, w)
    ys, ks, vs, ss, cs, fs = _layer(
        x_sample, cache_k[layer], cache_v[layer], state_ssm[layer], state_ssm_conv[layer],
        state_ffn_conv[layer], lam, lambda_init, w)
    stack = lambda a: a[None]
    return (yp, ys, stack(kp), stack(vp), stack(sp), stack(cp), stack(fp),
            stack(ks), stack(vs), stack(ss), stack(cs), stack(fs))
```

```python
import functools
import math

import jax
import jax.numpy as jnp
from jax import lax
from jax.experimental import pallas as pl
from jax.experimental.pallas import tpu as pltpu

F32 = jnp.float32
BF16 = jnp.bfloat16

D_MODEL = 1024
CHUNK = 64
NORM_EPS = 1e-6

SSM_D_INNER = 2048
SSM_HEAD_DIM = 64
SSM_HEADS = 32
SSM_GROUPS = 4
SSM_D_STATE = 128
SSM_CONV = 4
SSM_CONV_DIM = 3072
SSM_GROUP_WIDTH = SSM_D_INNER // SSM_GROUPS
SSM_PAIRS_PER_GROUP = SSM_GROUP_WIDTH // 128

ATT_HEADS = 8
ATT_HEAD_DIM = 64
ATT_WIDTH = 1024
ROPE_THETA = 500000.0
ROT_DIM = 16

D_FF = 2816
FFN_CONV = 3

IN_SIZES = (SSM_D_INNER, SSM_CONV_DIM, SSM_HEADS, ATT_WIDTH, ATT_WIDTH, ATT_WIDTH, D_MODEL, D_MODEL)

PROJ_WIDTH = 10240
COL_Z, COL_Q, COL_K, COL_V, COL_GS, COL_XBC, COL_GA = 0, 2048, 3072, 4096, 5120, 6144, 9216

LANES = 128
VT_ROWS = LANES + 16
NEG_BIG = -1e30
VMEM_LIMIT_MB = 56


def _cparams(semantics):
    return pltpu.CompilerParams(dimension_semantics=semantics,
                                vmem_limit_bytes=VMEM_LIMIT_MB * 1024 * 1024)


def _sigmoid(x):
    return 1.0 / (1.0 + jnp.exp(-x))


def _silu(x):
    h = 0.5 * x
    return h + h * jnp.tanh(h)


def _softplus(x):
    return jnp.maximum(x, 0.0) + jnp.log(1.0 + jnp.exp(-jnp.abs(x)))


def _inproj_kernel(x_ref, nw_ref, w_ref, wdt_ref, proj_ref, dt_ref, xn_ref):
    @pl.when(pl.program_id(1) == 0)
    def _():
        x = x_ref[...]
        xn = x * lax.rsqrt(jnp.mean(x * x, axis=-1, keepdims=True) + NORM_EPS) * nw_ref[...]
        xn_ref[...] = xn.astype(BF16)
        dt_ref[...] = jnp.dot(xn_ref[...], wdt_ref[...], preferred_element_type=F32)

    proj_ref[...] = jnp.dot(xn_ref[...], w_ref[...], preferred_element_type=F32)


def _in_proj(x2d, norm_w, w_main, w_dt):
    n = x2d.shape[0]
    tm = min(n, 1024)
    tn = 2048
    return pl.pallas_call(
        _inproj_kernel,
        grid=(n // tm, PROJ_WIDTH // tn),
        in_specs=[
            pl.BlockSpec((tm, D_MODEL), lambda i, j: (i, 0)),
            pl.BlockSpec((1, D_MODEL), lambda i, j: (0, 0)),
            pl.BlockSpec((D_MODEL, tn), lambda i, j: (0, j)),
            pl.BlockSpec((D_MODEL, LANES), lambda i, j: (0, 0)),
        ],
        out_specs=[
            pl.BlockSpec((tm, tn), lambda i, j: (i, j)),
            pl.BlockSpec((tm, LANES), lambda i, j: (i, 0)),
        ],
        out_shape=[
            jax.ShapeDtypeStruct((n, PROJ_WIDTH), F32),
            jax.ShapeDtypeStruct((n, LANES), F32),
        ],
        scratch_shapes=[pltpu.VMEM((tm, D_MODEL), BF16)],
        compiler_params=_cparams(("parallel", "arbitrary")),
        name="in_proj",
    )(x2d, norm_w, w_main, w_dt)


def _qkprep_kernel(q_ref, k_ref, v_ref, cos_ref, sn_ref, sp_ref, qw_ref, kw_ref, bd_ref,
                   qb_ref, kf_ref, kb_ref, vf_ref, vb_ref, *, q_scale, head_major):
    tm = q_ref.shape[0]
    bd = bd_ref[...]
    cos, sn, sp = cos_ref[...], sn_ref[...], sp_ref[...]

    def norm_rope(x, w):
        x2 = x * x
        hi = x2.astype(BF16)
        lo = (x2 - hi.astype(F32)).astype(BF16)
        ss = (jnp.dot(hi, bd, preferred_element_type=F32)
              + jnp.dot(lo, bd, preferred_element_type=F32))
        y = x * lax.rsqrt(ss * (1.0 / ATT_HEAD_DIM) + NORM_EPS) * w
        return (y * cos + pltpu.roll(y, LANES - ROT_DIM // 2, 1) * sn
                + pltpu.roll(y, ROT_DIM // 2, 1) * sp)

    for g in range(ATT_WIDTH // LANES):
        sl = slice(g * LANES, (g + 1) * LANES)
        qr = norm_rope(q_ref[:, sl], qw_ref[...])
        qb_ref[:, sl] = (qr * q_scale).astype(BF16)
        kr = norm_rope(k_ref[:, sl], kw_ref[...])
        kr_t = kr.T
        kf_ref[0, 2 * g] = kr_t[:ATT_HEAD_DIM]
        kf_ref[0, 2 * g + 1] = kr_t[ATT_HEAD_DIM:]
        v = v_ref[:, sl]
        vf_ref[pl.ds(g, tm, stride=ATT_HEADS), :] = v
        if head_major:
            kb_ref[0, g] = kr.astype(BF16)
            vb_ref[0, g, 0:LANES, :] = v.T.astype(BF16)
            extra = lax.broadcasted_iota(jnp.int32, (VT_ROWS - LANES, tm), 0) == 0
            vb_ref[0, g, LANES:VT_ROWS, :] = extra.astype(F32).astype(BF16)
        else:
            kb_ref[:, sl] = kr.astype(BF16)
            vb_ref[:, sl] = v.astype(BF16)


def _qk_prep(proj, tables, qw, kw, bd, t, head_major):
    n = proj.shape[0]
    tm = min(t, 512)
    nt = t // tm
    cos, sn, sp = tables
    wide = lambda c: pl.BlockSpec((tm, ATT_WIDTH), lambda i: (i, c // ATT_WIDTH))
    tab = pl.BlockSpec((tm, LANES), lambda i: (i % nt, 0))
    vec = pl.BlockSpec((1, LANES), lambda i: (0, 0))
    out = pl.BlockSpec((tm, ATT_WIDTH), lambda i: (i, 0))
    if head_major:
        kb_spec = pl.BlockSpec((1, ATT_HEADS, tm, LANES), lambda i: (i // nt, 0, i % nt, 0))
        kb_shape = jax.ShapeDtypeStruct((n // t, ATT_HEADS, t, LANES), BF16)
        vb_spec = pl.BlockSpec((1, ATT_HEADS, VT_ROWS, tm), lambda i: (i // nt, 0, 0, i % nt))
        vb_shape = jax.ShapeDtypeStruct((n // t, ATT_HEADS, VT_ROWS, t), BF16)
    else:
        kb_spec = vb_spec = out
        kb_shape = vb_shape = jax.ShapeDtypeStruct((n, ATT_WIDTH), BF16)
    q_scale = ATT_HEAD_DIM ** -0.5 * math.log2(math.e)
    return pl.pallas_call(
        functools.partial(_qkprep_kernel, q_scale=q_scale, head_major=head_major),
        grid=(n // tm,),
        in_specs=[wide(COL_Q), wide(COL_K), wide(COL_V), tab, tab, tab, vec, vec,
                  pl.BlockSpec((LANES, LANES), lambda i: (0, 0))],
        out_specs=[out,
                   pl.BlockSpec((1, 2 * ATT_HEADS, ATT_HEAD_DIM, tm), lambda i: (i // nt, 0, 0, i % nt)),
                   kb_spec,
                   pl.BlockSpec((tm * ATT_HEADS, LANES), lambda i: (i, 0)), vb_spec],
        out_shape=[
            jax.ShapeDtypeStruct((n, ATT_WIDTH), BF16),
            jax.ShapeDtypeStruct((n // t, 2 * ATT_HEADS, ATT_HEAD_DIM, t), F32),
            kb_shape,
            jax.ShapeDtypeStruct((n * ATT_HEADS, LANES), F32),
            vb_shape,
        ],
        compiler_params=_cparams(("parallel",)),
        name="qk_prep",
    )(proj, proj, proj, cos, sn, sp, qw, kw, bd)


def _rope_tables(t, past_len):
    half = ROT_DIM // 2
    inv_freq = jnp.power(ROPE_THETA, -jnp.arange(half, dtype=F32) * 2.0 / ROT_DIM)
    pos = past_len + jnp.arange(t, dtype=jnp.int32)
    ang = pos.astype(F32)[:, None] * inv_freq[None, :]
    cos, sin = jnp.cos(ang), jnp.sin(ang)
    ones = jnp.ones((t, ATT_HEAD_DIM - ROT_DIM), F32)
    zeros_h = jnp.zeros((t, half), F32)
    zeros_r = jnp.zeros((t, ATT_HEAD_DIM - ROT_DIM), F32)
    cos_h = jnp.concatenate([cos, cos, ones], axis=1)
    sn_h = jnp.concatenate([-sin, zeros_h, zeros_r], axis=1)
    sp_h = jnp.concatenate([zeros_h, sin, zeros_r], axis=1)
    two = lambda a: jnp.concatenate([a, a], axis=1)
    return two(cos_h), two(sn_h), two(sp_h)


def _lam_kernel(q1_ref, k1_ref, q2_ref, k2_ref, out_ref, *, lambda_init):
    a = jnp.sum(q1_ref[...] * k1_ref[...], axis=-1, keepdims=True)
    b = jnp.sum(q2_ref[...] * k2_ref[...], axis=-1, keepdims=True)
    out_ref[...] = jnp.exp(a) - jnp.exp(b) + lambda_init


def _lambda(q1, k1, q2, k2, lambda_init):
    return pl.pallas_call(
        functools.partial(_lam_kernel, lambda_init=lambda_init),
        out_shape=jax.ShapeDtypeStruct((1, 1), F32),
        name="lambda",
    )(q1, k1, q2, k2)


def _attn_kernel(lam_ref, q_ref, kp_ref, vp_ref, kn_ref, vn_ref, sw_ref, o_ref, qm_ref, m_ref, l_ref, acc_ref,
                 *, tq, tn, past_len, out_scale):
    j = pl.program_id(1)
    n_past = pl.num_programs(1) - 1

    @pl.when(j == 0)
    def _():
        lo_half = lax.broadcasted_iota(jnp.int32, (tq, LANES), 1) < ATT_HEAD_DIM
        for h in range(ATT_HEADS):
            qh = q_ref[0, :, h * LANES:(h + 1) * LANES]
            zero = jnp.zeros_like(qh)
            qm_ref[h, 0:tq, :] = jnp.where(lo_half, qh, zero)
            qm_ref[h, tq:2 * tq, :] = jnp.where(lo_half, zero, qh)
        m_ref[...] = jnp.full(m_ref.shape, NEG_BIG, F32)
        l_ref[...] = jnp.zeros(l_ref.shape, F32)
        acc_ref[...] = jnp.zeros(acc_ref.shape, F32)

    def step(head_k, keys_on_lanes, head_v, width, masked):
        if masked:
            row = lax.broadcasted_iota(jnp.int32, (2 * tq, width), 0)
            col = lax.broadcasted_iota(jnp.int32, (2 * tq, width), 1)
            visible = ((past_len + col) >> 6) <= ((past_len + (row & (tq - 1))) >> 6)
        for h in range(ATT_HEADS):
            s = lax.dot_general(qm_ref[h], head_k(h), (((1,), (0 if keys_on_lanes else 1,)), ((), ())),
                                preferred_element_type=F32)
            if masked:
                s = jnp.where(visible, s, NEG_BIG)
            m_prev = m_ref[h]
            m_new = jnp.maximum(m_prev, jnp.max(s, axis=-1, keepdims=True))
            alpha = jnp.exp2(m_prev - m_new)
            p = jnp.exp2(s - jnp.concatenate([m_new] * (width // LANES), axis=1))
            l_ref[h] = alpha * l_ref[h] + jnp.sum(p, axis=-1, keepdims=True)
            acc_ref[h] = acc_ref[h] * alpha + jnp.dot(p.astype(BF16), head_v(h), preferred_element_type=F32)
            m_ref[h] = m_new

    @pl.when(j < n_past)
    def _():
        step(lambda h: jnp.concatenate([kp_ref[0, 2 * h], kp_ref[0, 2 * h + 1]], axis=0).astype(BF16), True,
             lambda h: vp_ref[0, :, h, :].astype(BF16), kp_ref.shape[3], False)

    @pl.when(j == n_past)
    def _():
        step(lambda h: kn_ref[0, :, h * LANES:(h + 1) * LANES], False,
             lambda h: vn_ref[0, :, h * LANES:(h + 1) * LANES], tn, True)
        lam = lam_ref[0, 0]
        for h in range(ATT_HEADS):
            hs = slice(h * LANES, (h + 1) * LANES)
            o = acc_ref[h] / l_ref[h]
            o = o[0:tq] - lam * o[tq:2 * tq]
            on = o * lax.rsqrt(jnp.mean(o * o, axis=-1, keepdims=True) + NORM_EPS) * sw_ref[...]
            o_ref[0, :, hs] = (on * out_scale).astype(BF16)


def _attention(lam, q, k_past_t, v_past, k_new, v_new, subln_w, out_scale):
    b, t, _ = q.shape
    past_len = v_past.shape[1]
    tk = min(past_len, 1024)
    tn = -(-t // LANES) * LANES
    assert (t & (t - 1) == 0 and t % CHUNK == 0 and t <= 256 and past_len > 0 and past_len % tk == 0
            and past_len % CHUNK == 0 and tk % LANES == 0 and CHUNK == 64)
    n_past = past_len // tk
    if tn != t:
        k_new = jnp.pad(k_new, ((0, 0), (0, tn - t), (0, 0)))
        v_new = jnp.pad(v_new, ((0, 0), (0, tn - t), (0, 0)))
    past_k = pl.BlockSpec((1, 2 * ATT_HEADS, ATT_HEAD_DIM, tk),
                          lambda bi, j: (bi, 0, 0, jnp.minimum(j, n_past - 1)))
    past_v = pl.BlockSpec((1, tk, ATT_HEADS, 2 * ATT_HEAD_DIM),
                          lambda bi, j: (bi, jnp.minimum(j, n_past - 1), 0, 0))
    new = pl.BlockSpec((1, tn, ATT_WIDTH), lambda bi, j: (bi, 0, 0))
    return pl.pallas_call(
        functools.partial(_attn_kernel, tq=t, tn=tn, past_len=past_len, out_scale=out_scale),
        grid=(b, n_past + 1),
        in_specs=[
            pl.BlockSpec(memory_space=pltpu.SMEM),
            pl.BlockSpec((1, t, ATT_WIDTH), lambda bi, j: (bi, 0, 0)),
            past_k, past_v, new, new,
            pl.BlockSpec((1, LANES), lambda bi, j: (0, 0)),
        ],
        out_specs=pl.BlockSpec((1, t, ATT_WIDTH), lambda bi, j: (bi, 0, 0)),
        out_shape=jax.ShapeDtypeStruct((b, t, ATT_WIDTH), BF16),
        scratch_shapes=[
            pltpu.VMEM((ATT_HEADS, 2 * t, LANES), BF16),
            pltpu.VMEM((ATT_HEADS, 2 * t, LANES), F32),
            pltpu.VMEM((ATT_HEADS, 2 * t, LANES), F32),
            pltpu.VMEM((ATT_HEADS, 2 * t, LANES), F32),
        ],
        compiler_params=_cparams(("parallel", "arbitrary")),
        name="attention",
    )(lam, q, k_past_t, v_past, k_new, v_new, subln_w)


def _attn_t_kernel(qblk_ref, kblk_ref, lam_ref, q_ref, k_ref, vt_ref, swb_ref, o_ref,
                   qm_ref, m_ref, l_ref, acc_ref, s_ref, bmax_ref, *, tq, tk, past_len, out_scale):
    step = pl.program_id(1)
    i = qblk_ref[step]
    j = kblk_ref[step]
    q0 = past_len + i * tq
    j_last = (((q0 + tq - 1) // CHUNK) * CHUNK + CHUNK - 1) // tk
    col_tile = 2 * LANES

    @pl.when(j == 0)
    def _():
        lo_half = lax.broadcasted_iota(jnp.int32, (tq, LANES), 1) < ATT_HEAD_DIM
        for h in range(ATT_HEADS):
            qh = q_ref[0, :, h * LANES:(h + 1) * LANES]
            zero = jnp.zeros_like(qh)
            qm_ref[h, 0:tq, :] = jnp.where(lo_half, qh, zero)
            qm_ref[h, tq:2 * tq, :] = jnp.where(lo_half, zero, qh)
        m_ref[...] = jnp.full(m_ref.shape, NEG_BIG, F32)
        l_ref[...] = jnp.zeros(l_ref.shape, F32)
        acc_ref[...] = jnp.zeros(acc_ref.shape, F32)

    def block(masked):
        if masked:
            k_chunk = (j * tk + lax.broadcasted_iota(jnp.int32, (tk, 2 * tq), 0)) >> 6
            q_col = lax.broadcasted_iota(jnp.int32, (tk, 2 * tq), 1) & (tq - 1)
            visible = k_chunk <= ((q0 + q_col) >> 6)

        def scores(h, slot):
            st = lax.dot_general(k_ref[0, h], qm_ref[h], (((1,), (1,)), ((), ())),
                                 preferred_element_type=F32)
            if masked:
                st = jnp.where(visible, st, NEG_BIG)
            s_ref[slot] = st
            bmax_ref[slot] = jnp.max(st, axis=0, keepdims=True)

        def update(h, slot):
            m_prev = m_ref[h]
            m_new = jnp.maximum(m_prev, bmax_ref[slot])
            alpha = jnp.exp2(m_prev - m_new)
            m_ref[h] = m_new
            pvs = []
            for c in range(2 * tq // col_tile):
                cs = slice(c * col_tile, (c + 1) * col_tile)
                p = jnp.exp2(s_ref[slot, :, cs] - m_new[:, cs])
                pvs.append(jnp.dot(vt_ref[0, h], p.astype(BF16), preferred_element_type=F32))
            pv = jnp.concatenate(pvs, axis=1)
            l_ref[h] = alpha * l_ref[h] + pv[LANES:LANES + 1]
            acc_ref[h] = acc_ref[h] * alpha + pv[0:LANES]

        scores(0, 0)
        for h in range(ATT_HEADS):
            if h + 1 < ATT_HEADS:
                scores(h + 1, (h + 1) % 2)
            update(h, h % 2)

    needs_mask = (j * tk + tk - 1) // CHUNK > q0 // CHUNK
    pl.when(needs_mask)(functools.partial(block, True))
    pl.when(jnp.logical_not(needs_mask))(functools.partial(block, False))

    @pl.when(j == j_last)
    def _():
        lam = lam_ref[0, 0]
        swb = jnp.concatenate([swb_ref[...]] * (tq // LANES), axis=1)
        for h in range(ATT_HEADS):
            o = acc_ref[h] / l_ref[h]
            o = o[:, 0:tq] - lam * o[:, tq:2 * tq]
            on =o * lax.rsqrt(jnp.mean(o * o, axis=0, keepdims=True) + NORM_EPS) * swb
            o_ref[0, :, h * LANES:(h + 1) * LANES] = (on * out_scale).T.astype(BF16)


def _attention_t(lam, q, k, vt, subln_w, past_len, out_scale):
    b, t, _ = q.shape
    s_len = k.shape[2]
    tq = min(t, 512)
    tk = min(s_len, 512)
    assert tq & (tq - 1) == 0 and tq >= 2 * LANES and s_len % tk == 0 and t % tq == 0 and tk % LANES == 0
    nq = t // tq
    pairs = [(i, j) for i in range(nq)
             for j in range((((past_len + i * tq + tq - 1) // CHUNK) * CHUNK + CHUNK - 1) // tk + 1)]
    qblk = jnp.asarray([p[0] for p in pairs], jnp.int32)
    kblk = jnp.asarray([p[1] for p in pairs], jnp.int32)

    swb = jnp.broadcast_to(subln_w.reshape(LANES, 1), (LANES, LANES))
    grid_spec = pltpu.PrefetchScalarGridSpec(
        num_scalar_prefetch=2,
        grid=(b, len(pairs)),
        in_specs=[
            pl.BlockSpec(memory_space=pltpu.SMEM),
            pl.BlockSpec((1, tq, ATT_WIDTH), lambda bi, s, qb, kb: (bi, qb[s], 0)),
            pl.BlockSpec((1, ATT_HEADS, tk, LANES), lambda bi, s, qb, kb: (bi, 0, kb[s], 0)),
            pl.BlockSpec((1, ATT_HEADS, VT_ROWS, tk), lambda bi, s, qb, kb: (bi, 0, 0, kb[s])),
            pl.BlockSpec((LANES, LANES), lambda bi, s, qb, kb: (0, 0)),
        ],
        out_specs=pl.BlockSpec((1, tq, ATT_WIDTH), lambda bi, s, qb, kb: (bi, qb[s], 0)),
        scratch_shapes=[
            pltpu.VMEM((ATT_HEADS, 2 * tq, LANES), BF16),
            pltpu.VMEM((ATT_HEADS, 1, 2 * tq), F32),
            pltpu.VMEM((ATT_HEADS, 1, 2 * tq), F32),
            pltpu.VMEM((ATT_HEADS, LANES, 2 * tq), F32),
            pltpu.VMEM((2, tk, 2 * tq), F32),
            pltpu.VMEM((2, 1, 2 * tq), F32),
        ],
    )
    return pl.pallas_call(
        functools.partial(_attn_t_kernel, tq=tq, tk=tk, past_len=past_len, out_scale=out_scale),
        grid_spec=grid_spec,
        out_shape=jax.ShapeDtypeStruct((b, t, ATT_WIDTH), BF16),
        compiler_params=_cparams(("parallel", "arbitrary")),
        name="attention_t",
    )(qblk, kblk, lam, q, k, vt, swb)


def _ssd_kernel(xbc_ref, z_ref, dt_ref, cw_ref, cb_ref, dtb_ref, alog_ref, dexp_ref, nw_ref, tril_ref,
                cpast_ref, spast_ref, y_ref, st_ref, xpad_ref, xcv_ref, state_ref, *, tt):
    ti = pl.program_id(1)
    L = CHUNK
    hist = SSM_CONV - 1
    base = 8
    assert SSM_CONV == 4

    @pl.when(ti == 0)
    def _():
        xpad_ref[0:base, :] = jnp.zeros((base, SSM_CONV_DIM), F32)
        xpad_ref[base - hist:base, :] = cpast_ref[0]
        state_ref[...] = spast_ref[0]

    xpad_ref[base:base + tt, :] = xbc_ref[0]
    w0, w1, w2, w3 = (cw_ref[d:d + 1, :] for d in range(SSM_CONV))
    for c in range(tt // L):
        r = base + c * L
        xa = xpad_ref[r - 8:r + L, :]
        x1 = pltpu.roll(xa, 1, 0)
        v2 = pltpu.roll(w1 * xa + w0 * x1, 2, 0)
        acc = cb_ref[...] + w3 * xa[8:] + w2 * x1[8:] + v2[8:]
        xcv_ref[c * L:(c + 1) * L, :] = _silu(acc)
    xpad_ref[base - hist:base, :] = xpad_ref[base + tt - hist:base + tt, :]

    lane = lax.broadcasted_iota(jnp.int32, (L, LANES), 1)
    lo_half = lane < SSM_HEAD_DIM
    tril2 = (lane & (SSM_HEAD_DIM - 1)) <= lax.broadcasted_iota(jnp.int32, (L, LANES), 0)
    a_neg = -jnp.exp(alog_ref[...]) * math.log2(math.e)
    tril = tril_ref[...]
    b_off = SSM_D_INNER
    c_off = SSM_D_INNER + SSM_GROUPS * SSM_D_STATE

    def chunk(c, carry):
        r0 = pl.multiple_of(c * L, L)
        rows = pl.ds(r0, L)
        dtv = _softplus(dt_ref[0, rows, :] + dtb_ref[...])
        da = dtv * a_neg
        hi = da.astype(BF16)
        r1 = da - hi.astype(F32)
        mid = r1.astype(BF16)
        lo = (r1 - mid.astype(F32)).astype(BF16)
        acum = (jnp.dot(tril, hi, preferred_element_type=F32)
                + jnp.dot(tril, mid, preferred_element_type=F32)
                + jnp.dot(tril, lo, preferred_element_type=F32))
        acum_t = jnp.concatenate([acum, acum], axis=0).T
        for g in range(SSM_GROUPS):
            gs = slice(g * SSM_GROUP_WIDTH, (g + 1) * SSM_GROUP_WIDTH)
            bm = xcv_ref[rows, b_off + g * SSM_D_STATE:b_off + (g + 1) * SSM_D_STATE].astype(BF16)
            cm = xcv_ref[rows, c_off + g * SSM_D_STATE:c_off + (g + 1) * SSM_D_STATE].astype(BF16)
            cb2 = lax.dot_general(cm, jnp.concatenate([bm, bm], axis=0), (((1,), (1,)), ((), ())),
                                  preferred_element_type=F32)
            st = state_ref[g]
            y_in = jnp.dot(cm, st.astype(BF16), preferred_element_type=F32)
            ys, xws, lasts = [], [], []
            for pp in range(SSM_PAIRS_PER_GROUP):
                pair = g * SSM_PAIRS_PER_GROUP + pp
                h1, h2 = 2 * pair, 2 * pair + 1
                ps = slice(pair * LANES, (pair + 1) * LANES)
                a_col = jnp.where(lo_half, acum[:, h1:h1 + 1], acum[:, h2:h2 + 1])
                dt_col = jnp.where(lo_half, dtv[:, h1:h1 + 1], dtv[:, h2:h2 + 1])
                a_row = jnp.where(lo_half[0:1], acum_t[h1:h1 + 1, :], acum_t[h2:h2 + 1, :])
                last = a_col[L - 1:L, :]
                xs = xcv_ref[rows, ps]
                xdt = xs * dt_col
                w = (cb2 * jnp.exp2(jnp.where(tril2, a_col - a_row, NEG_BIG))).astype(BF16)
                zero = jnp.zeros_like(xdt)
                rhs = jnp.concatenate([jnp.where(lo_half, xdt, zero), jnp.where(lo_half, zero, xdt)],
                                      axis=0).astype(BF16)
                y = (jnp.dot(w, rhs, preferred_element_type=F32)
                     + y_in[:, pp * LANES:(pp + 1) * LANES] * jnp.exp2(a_col)
                     + dexp_ref[:, ps] * xs)
                ys.append(y)
                xws.append(xdt * jnp.exp2(last - a_col))
                lasts.append(last)
            yg = jnp.concatenate(ys, axis=1)
            xwg = jnp.concatenate(xws, axis=1).astype(BF16)
            lastg = jnp.concatenate(lasts, axis=1)
            state_ref[g] = st * jnp.exp2(lastg) + lax.dot_general(
                bm, xwg, (((0,), (0,)), ((), ())), preferred_element_type=F32)
            yg = yg * _silu(z_ref[0, rows, gs])
            yn = yg * lax.rsqrt(jnp.mean(yg * yg, axis=-1, keepdims=True) + NORM_EPS) * nw_ref[:, gs]
            y_ref[0, rows, gs] = yn.astype(BF16)
        return carry

    lax.fori_loop(0, tt // L, chunk, 0, unroll=True)

    @pl.when(ti == pl.num_programs(1) - 1)
    def _():
        st_ref[0] = state_ref[...]


def _ssd(proj3, dt3, conv_past, state_past_t, cw, cb, dtb, alog, dexp, nw, tril):
    b, t, _ = proj3.shape
    tt = min(t, 256)
    full = lambda shape: pl.BlockSpec(shape, lambda bi, i: (0,) * len(shape))
    return pl.pallas_call(
        functools.partial(_ssd_kernel, tt=tt),
        grid=(b, t // tt),
        in_specs=[
            pl.BlockSpec((1, tt, SSM_CONV_DIM), lambda bi, i: (bi, i, COL_XBC // SSM_CONV_DIM)),
            pl.BlockSpec((1, tt, SSM_D_INNER), lambda bi, i: (bi, i, COL_Z // SSM_D_INNER)),
            pl.BlockSpec((1, tt, LANES), lambda bi, i: (bi, i, 0)),
            full((SSM_CONV, SSM_CONV_DIM)), full((1, SSM_CONV_DIM)), full((1, LANES)), full((1, LANES)),
            full((1, SSM_D_INNER)), full((1, SSM_D_INNER)), full((CHUNK, CHUNK)),
            pl.BlockSpec((1, SSM_CONV - 1, SSM_CONV_DIM), lambda bi, i: (bi, 0, 0)),
            pl.BlockSpec((1, SSM_GROUPS, SSM_D_STATE, SSM_GROUP_WIDTH), lambda bi, i: (bi, 0, 0, 0)),
        ],
        out_specs=[
            pl.BlockSpec((1, tt, SSM_D_INNER), lambda bi, i: (bi, i, 0)),
            pl.BlockSpec((1, SSM_GROUPS, SSM_D_STATE, SSM_GROUP_WIDTH), lambda bi, i: (bi, 0, 0, 0)),
        ],
        out_shape=[
            jax.ShapeDtypeStruct((b, t, SSM_D_INNER), BF16),
            jax.ShapeDtypeStruct((b, SSM_GROUPS, SSM_D_STATE, SSM_GROUP_WIDTH), F32),
        ],
        scratch_shapes=[
            pltpu.VMEM((tt + 8, SSM_CONV_DIM), F32),
            pltpu.VMEM((tt, SSM_CONV_DIM), F32),
            pltpu.VMEM((SSM_GROUPS, SSM_D_STATE, SSM_GROUP_WIDTH), F32),
        ],
        compiler_params=_cparams(("parallel", "arbitrary")),
        name="ssd",
    )(proj3, proj3, dt3, cw, cb, dtb, alog, dexp, nw, tril, conv_past, state_past_t)


def _merge_kernel(x_ref, ys_ref, o_ref, gs_ref, ga_ref, wbs_ref, wba_ref, wo_ref, nfw_ref, x1_ref, hn_ref):
    bs = jnp.dot(ys_ref[...], wbs_ref[...], preferred_element_type=F32)
    ba = jnp.dot(o_ref[...], wba_ref[...], preferred_element_type=F32)
    mix = _sigmoid(gs_ref[...]) * bs + _sigmoid(ga_ref[...]) * ba
    x1 = x_ref[...] + jnp.dot(mix.astype(BF16), wo_ref[...], preferred_element_type=F32)
    x1_ref[...] = x1
    hn = x1 * lax.rsqrt(jnp.mean(x1 * x1, axis=-1, keepdims=True) + NORM_EPS) * nfw_ref[...]
    hn_ref[...] = hn.astype(BF16)


def _merge(x2d, ys, o, proj, wbs, wba, wo, nfw):
    n = x2d.shape[0]
    tm = min(n, 512)
    row = lambda w, c=0: pl.BlockSpec((tm, w), lambda i: (i, c))
    full = lambda shape: pl.BlockSpec(shape, lambda i: (0, 0))
    return pl.pallas_call(
        _merge_kernel,
        grid=(n // tm,),
        in_specs=[row(D_MODEL), row(SSM_D_INNER), row(ATT_WIDTH),
                  row(D_MODEL, COL_GS // D_MODEL), row(D_MODEL, COL_GA // D_MODEL),
                  full((SSM_D_INNER, D_MODEL)), full((ATT_WIDTH, D_MODEL)), full((D_MODEL, D_MODEL)),
                  full((1, D_MODEL))],
        out_specs=[row(D_MODEL), row(D_MODEL)],
        out_shape=[jax.ShapeDtypeStruct((n, D_MODEL), F32), jax.ShapeDtypeStruct((n, D_MODEL), BF16)],
        compiler_params=_cparams(("parallel",)),
        name="merge",
    )(x2d, ys, o, proj, proj, wbs, wba, wo, nfw)


FFN_SPLIT = 2


def _ffn_kernel(hn_ref, x1_ref, wa_ref, wb_ref, wd_ref, cw_ref, cb_ref, past_ref, out_ref, tail_ref,
                hbuf_ref, *, tm):
    ti = pl.program_id(1)
    hist = FFN_CONV - 1
    base = 8
    tf = D_FF // FFN_SPLIT

    @pl.when(ti == 0)
    def _():
        hbuf_ref[base - hist:base, :] = past_ref[0]

    hn = hn_ref[0]
    acc = x1_ref[0]
    for f in range(FFN_SPLIT):
        fs = slice(f * tf, (f + 1) * tf)
        ha = jnp.dot(hn, wa_ref[:, fs], preferred_element_type=F32)
        hb = jnp.dot(hn, wb_ref[:, fs], preferred_element_type=F32)
        hbuf_ref[base:base + tm, fs] = ha
        hc = cb_ref[:, fs] + cw_ref[hist:hist + 1, fs] * ha
        for d in range(1, FFN_CONV):
            hc = hc + cw_ref[hist - d:hist - d + 1, fs] * hbuf_ref[base - d:base - d + tm, fs]
        u = (_silu(hc) * hb).astype(BF16)
        acc = acc + jnp.dot(u, wd_ref[fs, :], preferred_element_type=F32)
    out_ref[0] = acc
    tail = hbuf_ref[base + tm - hist:base + tm, :]
    hbuf_ref[base - hist:base, :] = tail
    tail_ref[0] = tail


def _ffn(hn3, x13, wa, wb, wd, cw, cb, past):
    b, t, _ = hn3.shape
    tm = min(t, 512)
    const = lambda shape: pl.BlockSpec(shape, lambda bi, i: (0, 0), pipeline_mode=pl.Buffered(1))
    return pl.pallas_call(
        functools.partial(_ffn_kernel, tm=tm),
        grid=(b, t // tm),
        in_specs=[
            pl.BlockSpec((1, tm, D_MODEL), lambda bi, i: (bi, i, 0)),
            pl.BlockSpec((1, tm, D_MODEL), lambda bi, i: (bi, i, 0)),
            const((D_MODEL, D_FF)), const((D_MODEL, D_FF)), const((D_FF, D_MODEL)),
            const((FFN_CONV, D_FF)), const((1, D_FF)),
            pl.BlockSpec((1, FFN_CONV - 1, D_FF), lambda bi, i: (bi, 0, 0)),
        ],
        out_specs=[
            pl.BlockSpec((1, tm, D_MODEL), lambda bi, i: (bi, i, 0)),
            pl.BlockSpec((1, FFN_CONV - 1, D_FF), lambda bi, i: (bi, 0, 0)),
        ],
        out_shape=[
            jax.ShapeDtypeStruct((b, t, D_MODEL), F32),
            jax.ShapeDtypeStruct((b, FFN_CONV - 1, D_FF), F32),
        ],
        scratch_shapes=[pltpu.VMEM((tm + 8, D_FF), F32)],
        compiler_params=_cparams(("parallel", "arbitrary")),
        name="ffn",
    )(hn3, x13, wa, wb, wd, cw, cb, past)


def _prepare_weights(norm_mix_w, w_in, ssm_conv_w, ssm_conv_b, ssm_dt_bias, ssm_a_log, ssm_d, ssm_norm_w,
                     q_norm_w, k_norm_w, subln_w, w_branch_ssm, w_branch_attn, w_out, norm_ffn_w, w_up,
                     ffn_conv_w, ffn_conv_b, w_down):
    bounds = [0]
    for s in IN_SIZES:
        bounds.append(bounds[-1] + s)
    piece = lambda n: w_in[:, bounds[n]:bounds[n + 1]]
    z, xbc, dt, q, k, v, gs, ga = (piece(n) for n in range(8))
    pad_heads = lambda a: jnp.pad(a.reshape(1, SSM_HEADS), ((0, 0), (0, LANES - SSM_HEADS)))
    row = lambda a: a.reshape(1, -1)
    two_heads = lambda a: jnp.concatenate([a, a]).reshape(1, LANES)
    head = jnp.arange(LANES) // ATT_HEAD_DIM
    tri = jnp.arange(CHUNK)
    return dict(
        norm_mix=row(norm_mix_w),
        w_main=jnp.concatenate([z, q, k, v, gs, xbc, ga], axis=1).astype(BF16),
        w_dt=jnp.pad(dt, ((0, 0), (0, LANES - SSM_HEADS))).astype(BF16),
        conv_w=ssm_conv_w, conv_b=row(ssm_conv_b),
        dt_bias=pad_heads(ssm_dt_bias), a_log=pad_heads(ssm_a_log),
        d_exp=row(jnp.repeat(ssm_d, SSM_HEAD_DIM)), ssm_norm=row(ssm_norm_w),
        tril=(tri[:, None] >= tri[None, :]).astype(BF16),
        q_norm=two_heads(q_norm_w), k_norm=two_heads(k_norm_w),
        head_ones=(head[:, None] == head[None, :]).astype(BF16),
        subln=row(subln_w),
        w_bs=w_branch_ssm.astype(BF16), w_ba=w_branch_attn.astype(BF16), w_out=w_out.astype(BF16),
        norm_ffn=row(norm_ffn_w),
        w_up_a=w_up[:, :D_FF].astype(BF16), w_up_b=w_up[:, D_FF:].astype(BF16),
        w_down=w_down.astype(BF16), ffn_conv_w=ffn_conv_w, ffn_conv_b=row(ffn_conv_b),
    )


def _state_to_kernel_layout(s):
    b = s.shape[0]
    s = s.reshape(b, SSM_GROUPS, SSM_HEADS // SSM_GROUPS, SSM_HEAD_DIM, SSM_D_STATE)
    return jnp.transpose(s, (0, 1, 4, 2, 3)).reshape(b, SSM_GROUPS, SSM_D_STATE, SSM_GROUP_WIDTH)


def _state_from_kernel_layout(s):
    b = s.shape[0]
    s = s.reshape(b, SSM_GROUPS, SSM_D_STATE, SSM_HEADS // SSM_GROUPS, SSM_HEAD_DIM)
    return jnp.transpose(s, (0, 1, 3, 4, 2)).reshape(b, SSM_HEADS, SSM_HEAD_DIM, SSM_D_STATE)


def _layer(x, k_past, v_past, ssm_past, conv_past, ffn_past, lam, lambda_init, w):
    b, t, _ = x.shape
    past_len = k_past.shape[1]
    n = b * t
    x2d = x.reshape(n, D_MODEL)

    proj, dt = _in_proj(x2d, w["norm_mix"], w["w_main"], w["w_dt"])
    keys_on_rows = past_len == 0 and t % (4 * LANES) == 0
    qb, kf, kb, vf, vb = _qk_prep(proj, _rope_tables(t, past_len), w["q_norm"], w["k_norm"],
                                  w["head_ones"], t, keys_on_rows)
    q3 = qb.reshape(b, t, ATT_WIDTH)
    if keys_on_rows:
        o = _attention_t(lam, q3, kb, vb, w["subln"], past_len, 1.0 - lambda_init)
    else:
        o = _attention(lam, q3, jnp.transpose(k_past, (0, 2, 3, 1)), v_past, kb.reshape(b, t, ATT_WIDTH),
                       vb.reshape(b, t, ATT_WIDTH), w["subln"], 1.0 - lambda_init)

    ys, state_t = _ssd(proj.reshape(b, t, PROJ_WIDTH), dt.reshape(b, t, LANES), conv_past,
                       _state_to_kernel_layout(ssm_past), w["conv_w"], w["conv_b"], w["dt_bias"],
                       w["a_log"], w["d_exp"], w["ssm_norm"], w["tril"])

    x1, hn = _merge(x2d, ys.reshape(n, SSM_D_INNER), o.reshape(n, ATT_WIDTH), proj,
                    w["w_bs"], w["w_ba"], w["w_out"], w["norm_ffn"])
    y, ffn_new = _ffn(hn.reshape(b, t, D_MODEL), x1.reshape(b, t, D_MODEL), w["w_up_a"], w["w_up_b"],
                      w["w_down"], w["ffn_conv_w"], w["ffn_conv_b"], ffn_past)

    xbc_rows = proj.reshape(b, t, PROJ_WIDTH)[:, t - (SSM_CONV - 1):, COL_XBC:COL_XBC + SSM_CONV_DIM]
    k_new = jnp.transpose(kf, (0, 3, 1, 2))
    v_new = vf.reshape(b, t, ATT_HEADS, 2 * ATT_HEAD_DIM)
    return y, k_new, v_new, _state_from_kernel_layout(state_t), xbc_rows, ffn_new


def kernel(x_prompt, x_sample, cache_k, cache_v, state_ssm, state_ssm_conv, state_ffn_conv, norm_mix_w, w_in, ssm_conv_w, ssm_conv_b, ssm_dt_bias, ssm_a_log, ssm_d, ssm_norm_w, q_norm_w, k_norm_w, lambda_q1, lambda_k1, lambda_q2, lambda_k2, subln_w, w_branch_ssm, w_branch_attn, w_out, norm_ffn_w, w_up, ffn_conv_w, ffn_conv_b, w_down):
    depth = w_in.shape[0]
    assert depth == 1
    bp = x_prompt.shape[0]
    dt_ = x_prompt.dtype
    layer = 0
    lambda_init = 0.8 - 0.6 * math.exp(-0.3 * layer)
    w = _prepare_weights(norm_mix_w[layer], w_in[layer], ssm_conv_w[layer], ssm_conv_b[layer],
                         ssm_dt_bias[layer], ssm_a_log[layer], ssm_d[layer], ssm_norm_w[layer],
                         q_norm_w[layer], k_norm_w[layer], subln_w[layer], w_branch_ssm[layer],
                         w_branch_attn[layer], w_out[layer], norm_ffn_w[layer], w_up[layer],
                         ffn_conv_w[layer], ffn_conv_b[layer], w_down[layer])
    lam = _lambda(lambda_q1[layer].reshape(1, -1), lambda_k1[layer].reshape(1, -1),
                  lambda_q2[layer].reshape(1, -1), lambda_k2[layer].reshape(1, -1), lambda_init)

    yp, kp, vp, sp, cp, fp = _layer(
        x_prompt,
        jnp.zeros((bp, 0, 2 * ATT_HEADS, ATT_HEAD_DIM), dt_),
        jnp.zeros((bp, 0, ATT_HEADS, 2 * ATT_HEAD_DIM), dt_),
        jnp.zeros((bp, SSM_HEADS, SSM_HEAD_DIM, SSM_D_STATE), dt_),
        jnp.zeros((bp, SSM_CONV - 1, SSM_CONV_DIM), dt_),
        jnp.zeros((bp, FFN_CONV - 1, D_FF), dt_),
        lam, lambda_init, w)
    ys, ks, vs, ss, cs, fs = _layer(
        x_sample, cache_k[layer], cache_v[layer], state_ssm[layer], state_ssm_conv[layer],
        state_ffn_conv[layer], lam, lambda_init, w)
    stack = lambda a: a[None]
    return (yp, ys, stack(kp), stack(vp), stack(sp), stack(cp), stack(fp),
            stack(ks), stack(vs), stack(ss), stack(cs), stack(fs))
```

```python
import functools
import math

import jax
import jax.numpy as jnp
from jax import lax
from jax.experimental import pallas as pl
from jax.experimental.pallas import tpu as pltpu

F32 = jnp.float32
BF16 = jnp.bfloat16

D_MODEL = 1024
CHUNK = 64
NORM_EPS = 1e-6

SSM_D_INNER = 2048
SSM_HEAD_DIM = 64
SSM_HEADS = 32
SSM_GROUPS = 4
SSM_D_STATE = 128
SSM_CONV = 4
SSM_CONV_DIM = 3072
SSM_GROUP_WIDTH = SSM_D_INNER // SSM_GROUPS
SSM_PAIRS_PER_GROUP = SSM_GROUP_WIDTH // 128

ATT_HEADS = 8
ATT_HEAD_DIM = 64
ATT_WIDTH = 1024
ROPE_THETA = 500000.0
ROT_DIM = 16

D_FF = 2816
FFN_CONV = 3

IN_SIZES = (SSM_D_INNER, SSM_CONV_DIM, SSM_HEADS, ATT_WIDTH, ATT_WIDTH, ATT_WIDTH, D_MODEL, D_MODEL)

PROJ_WIDTH = 10240
COL_Z, COL_Q, COL_K, COL_V, COL_GS, COL_XBC, COL_GA = 0, 2048, 3072, 4096, 5120, 6144, 9216

LANES = 128
VT_ROWS = LANES + 16
NEG_BIG = -1e30
VMEM_LIMIT_MB = 56


def _cparams(semantics):
    return pltpu.CompilerParams(dimension_semantics=semantics,
                                vmem_limit_bytes=VMEM_LIMIT_MB * 1024 * 1024)


def _sigmoid(x):
    return 1.0 / (1.0 + jnp.exp(-x))


def _silu(x):
    h = 0.5 * x
    return h + h * jnp.tanh(h)


def _softplus(x):
    return jnp.maximum(x, 0.0) + jnp.log(1.0 + jnp.exp(-jnp.abs(x)))


def _inproj_kernel(x_ref, nw_ref, w_ref, wdt_ref, proj_ref, dt_ref, xn_ref):
    @pl.when(pl.program_id(1) == 0)
    def _():
        x = x_ref[...]
        xn = x * lax.rsqrt(jnp.mean(x * x, axis=-1, keepdims=True) + NORM_EPS) * nw_ref[...]
        xn_ref[...] = xn.astype(BF16)
        dt_ref[...] = jnp.dot(xn_ref[...], wdt_ref[...], preferred_element_type=F32)

    proj_ref[...] = jnp.dot(xn_ref[...], w_ref[...], preferred_element_type=F32)


def _in_proj(x2d, norm_w, w_main, w_dt):
    n = x2d.shape[0]
    tm = min(n, 1024)
    tn = 2048
    return pl.pallas_call(
        _inproj_kernel,
        grid=(n // tm, PROJ_WIDTH // tn),
        in_specs=[
            pl.BlockSpec((tm, D_MODEL), lambda i, j: (i, 0)),
            pl.BlockSpec((1, D_MODEL), lambda i, j: (0, 0)),
            pl.BlockSpec((D_MODEL, tn), lambda i, j: (0, j)),
            pl.BlockSpec((D_MODEL, LANES), lambda i, j: (0, 0)),
        ],
        out_specs=[
            pl.BlockSpec((tm, tn), lambda i, j: (i, j)),
            pl.BlockSpec((tm, LANES), lambda i, j: (i, 0)),
        ],
        out_shape=[
            jax.ShapeDtypeStruct((n, PROJ_WIDTH), F32),
            jax.ShapeDtypeStruct((n, LANES), F32),
        ],
        scratch_shapes=[pltpu.VMEM((tm, D_MODEL), BF16)],
        compiler_params=_cparams(("parallel", "arbitrary")),
        name="in_proj",
    )(x2d, norm_w, w_main, w_dt)


def _qkprep_kernel(q_ref, k_ref, v_ref, cos_ref, sn_ref, sp_ref, qw_ref, kw_ref, bd_ref,
                   qb_ref, kf_ref, kb_ref, vf_ref, vb_ref, *, q_scale, head_major):
    tm = q_ref.shape[0]
    bd = bd_ref[...]
    cos, sn, sp = cos_ref[...], sn_ref[...], sp_ref[...]

    def norm_rope(x, w):
        x2 = x * x
        hi = x2.astype(BF16)
        lo = (x2 - hi.astype(F32)).astype(BF16)
        ss = (jnp.dot(hi, bd, preferred_element_type=F32)
              + jnp.dot(lo, bd, preferred_element_type=F32))
        y = x * lax.rsqrt(ss * (1.0 / ATT_HEAD_DIM) + NORM_EPS) * w
        return (y * cos + pltpu.roll(y, LANES - ROT_DIM // 2, 1) * sn
                + pltpu.roll(y, ROT_DIM // 2, 1) * sp)

    for g in range(ATT_WIDTH // LANES):
        sl = slice(g * LANES, (g + 1) * LANES)
        qr = norm_rope(q_ref[:, sl], qw_ref[...])
        qb_ref[:, sl] = (qr * q_scale).astype(BF16)
        kr = norm_rope(k_ref[:, sl], kw_ref[...])
        kr_t = kr.T
        kf_ref[0, 2 * g] = kr_t[:ATT_HEAD_DIM]
        kf_ref[0, 2 * g + 1] = kr_t[ATT_HEAD_DIM:]
        v = v_ref[:, sl]
        vf_ref[pl.ds(g, tm, stride=ATT_HEADS), :] = v
        if head_major:
            kb_ref[0, g] = kr.astype(BF16)
            vb_ref[0, g, 0:LANES, :] = v.T.astype(BF16)
            extra = lax.broadcasted_iota(jnp.int32, (VT_ROWS - LANES, tm), 0) == 0
            vb_ref[0, g, LANES:VT_ROWS, :] = extra.astype(F32).astype(BF16)
        else:
            kb_ref[:, sl] = kr.astype(BF16)
            vb_ref[:, sl] = v.astype(BF16)


def _qk_prep(proj, tables, qw, kw, bd, t, head_major):
    n = proj.shape[0]
    tm = min(t, 512)
    nt = t // tm
    cos, sn, sp = tables
    wide = lambda c: pl.BlockSpec((tm, ATT_WIDTH), lambda i: (i, c // ATT_WIDTH))
    tab = pl.BlockSpec((tm, LANES), lambda i: (i % nt, 0))
    vec = pl.BlockSpec((1, LANES), lambda i: (0, 0))
    out = pl.BlockSpec((tm, ATT_WIDTH), lambda i: (i, 0))
    if head_major:
        kb_spec = pl.BlockSpec((1, ATT_HEADS, tm, LANES), lambda i: (i // nt, 0, i % nt, 0))
        kb_shape = jax.ShapeDtypeStruct((n // t, ATT_HEADS, t, LANES), BF16)
        vb_spec = pl.BlockSpec((1, ATT_HEADS, VT_ROWS, tm), lambda i: (i // nt, 0, 0, i % nt))
        vb_shape = jax.ShapeDtypeStruct((n // t, ATT_HEADS, VT_ROWS, t), BF16)
    else:
        kb_spec = vb_spec = out
        kb_shape = vb_shape = jax.ShapeDtypeStruct((n, ATT_WIDTH), BF16)
    q_scale = ATT_HEAD_DIM ** -0.5 * math.log2(math.e)
    return pl.pallas_call(
        functools.partial(_qkprep_kernel, q_scale=q_scale, head_major=head_major),
        grid=(n // tm,),
        in_specs=[wide(COL_Q), wide(COL_K), wide(COL_V), tab, tab, tab, vec, vec,
                  pl.BlockSpec((LANES, LANES), lambda i: (0, 0))],
        out_specs=[out,
                   pl.BlockSpec((1, 2 * ATT_HEADS, ATT_HEAD_DIM, tm), lambda i: (i // nt, 0, 0, i % nt)),
                   kb_spec,
                   pl.BlockSpec((tm * ATT_HEADS, LANES), lambda i: (i, 0)), vb_spec],
        out_shape=[
            jax.ShapeDtypeStruct((n, ATT_WIDTH), BF16),
            jax.ShapeDtypeStruct((n // t, 2 * ATT_HEADS, ATT_HEAD_DIM, t), F32),
            kb_shape,
            jax.ShapeDtypeStruct((n * ATT_HEADS, LANES), F32),
            vb_shape,
        ],
        compiler_params=_cparams(("parallel",)),
        name="qk_prep",
    )(proj, proj, proj, cos, sn, sp, qw, kw, bd)


def _rope_tables(t, past_len):
    half = ROT_DIM // 2
    inv_freq = jnp.power(ROPE_THETA, -jnp.arange(half, dtype=F32) * 2.0 / ROT_DIM)
    pos = past_len + jnp.arange(t, dtype=jnp.int32)
    ang = pos.astype(F32)[:, None] * inv_freq[None, :]
    cos, sin = jnp.cos(ang), jnp.sin(ang)
    ones = jnp.ones((t, ATT_HEAD_DIM - ROT_DIM), F32)
    zeros_h = jnp.zeros((t, half), F32)
    zeros_r = jnp.zeros((t, ATT_HEAD_DIM - ROT_DIM), F32)
    cos_h = jnp.concatenate([cos, cos, ones], axis=1)
    sn_h = jnp.concatenate([-sin, zeros_h, zeros_r], axis=1)
    sp_h = jnp.concatenate([zeros_h, sin, zeros_r], axis=1)
    two = lambda a: jnp.concatenate([a, a], axis=1)
    return two(cos_h), two(sn_h), two(sp_h)


def _lam_kernel(q1_ref, k1_ref, q2_ref, k2_ref, out_ref, *, lambda_init):
    a = jnp.sum(q1_ref[...] * k1_ref[...], axis=-1, keepdims=True)
    b = jnp.sum(q2_ref[...] * k2_ref[...], axis=-1, keepdims=True)
    out_ref[...] = jnp.exp(a) - jnp.exp(b) + lambda_init


def _lambda(q1, k1, q2, k2, lambda_init):
    return pl.pallas_call(
        functools.partial(_lam_kernel, lambda_init=lambda_init),
        out_shape=jax.ShapeDtypeStruct((1, 1), F32),
        name="lambda",
    )(q1, k1, q2, k2)


def _attn_kernel(lam_ref, q_ref, kp_ref, vp_ref, kn_ref, vn_ref, sw_ref, o_ref, qm_ref, m_ref, l_ref, acc_ref,
                 *, tq, tn, past_len, out_scale):
    j = pl.program_id(1)
    n_past = pl.num_programs(1) - 1
    items = [(s, h) for s in range(q_ref.shape[0]) for h in range(ATT_HEADS)]

    @pl.when(j == 0)
    def _():
        lo_half = lax.broadcasted_iota(jnp.int32, (tq, LANES), 1) < ATT_HEAD_DIM
        for n, (s, h) in enumerate(items):
            qh = q_ref[s, :, h * LANES:(h + 1) * LANES]
            zero = jnp.zeros_like(qh)
            qm_ref[n, 0:tq, :] = jnp.where(lo_half, qh, zero)
            qm_ref[n, tq:2 * tq, :] = jnp.where(lo_half, zero, qh)
        m_ref[...] = jnp.full(m_ref.shape, NEG_BIG, F32)
        l_ref[...] = jnp.zeros(l_ref.shape, F32)
        acc_ref[...] = jnp.zeros(acc_ref.shape, F32)

    def step(head_k, keys_on_lanes, head_v, width, masked):
        if masked:
            row = lax.broadcasted_iota(jnp.int32, (2 * tq, width), 0)
            col = lax.broadcasted_iota(jnp.int32, (2 * tq, width), 1)
            visible = ((past_len + col) >> 6) <= ((past_len + (row & (tq - 1))) >> 6)
        for n, (s, h) in enumerate(items):
            sc = lax.dot_general(qm_ref[n], head_k(s, h), (((1,), (0 if keys_on_lanes else 1,)), ((), ())),
                                 preferred_element_type=F32)
            if masked:
                sc = jnp.where(visible, sc, NEG_BIG)
            m_prev = m_ref[n]
            m_new = jnp.maximum(m_prev, jnp.max(sc, axis=-1, keepdims=True))
            alpha = jnp.exp2(m_prev - m_new)
            p = jnp.exp2(sc - jnp.concatenate([m_new] * (width // LANES), axis=1))
            l_ref[n] = alpha * l_ref[n] + jnp.sum(p, axis=-1, keepdims=True)
            acc_ref[n] = acc_ref[n] * alpha + jnp.dot(p.astype(BF16), head_v(s, h),
                                                      preferred_element_type=F32)
            m_ref[n] = m_new

    @pl.when(j < n_past)
    def _():
        width = kp_ref.shape[3]
        step(lambda s, h: jnp.concatenate([kp_ref[s, 2 * h], kp_ref[s, 2 * h + 1]], axis=0).astype(BF16), True,
             lambda s, h: vp_ref[s, pl.ds(h, width, stride=ATT_HEADS), :].astype(BF16), width, False)

    @pl.when(j == n_past)
    def _():
        step(lambda s, h: kn_ref[s, :, h * LANES:(h + 1) * LANES], False,
             lambda s, h: vn_ref[s, :, h * LANES:(h + 1) * LANES], tn, True)
        lam = lam_ref[0, 0]
        for n, (s, h) in enumerate(items):
            hs = slice(h * LANES, (h + 1) * LANES)
            o = acc_ref[n] / l_ref[n]
            o = o[0:tq] - lam * o[tq:2 * tq]
            on = o * lax.rsqrt(jnp.mean(o * o, axis=-1, keepdims=True) + NORM_EPS) * sw_ref[...]
            o_ref[s, :, hs] = (on * out_scale).astype(BF16)


def _attention(lam, q, k_past_t, v_past, k_new, v_new, subln_w, out_scale):
    b, t, _ = q.shape
    past_len = v_past.shape[1]
    tk = min(past_len, 1024)
    tn = -(-t // LANES) * LANES
    assert (t & (t - 1) == 0 and t % CHUNK == 0 and t <= 256 and past_len > 0 and past_len % tk == 0
            and past_len % CHUNK == 0 and tk % LANES == 0 and CHUNK == 64)
    n_past = past_len // tk
    if tn != t:
        k_new = jnp.pad(k_new, ((0, 0), (0, tn - t), (0, 0)))
        v_new = jnp.pad(v_new, ((0, 0), (0, tn - t), (0, 0)))
    nb = 2 if b % 2 == 0 else 1
    n_items = nb * ATT_HEADS
    past_k = pl.BlockSpec((nb, 2 * ATT_HEADS, ATT_HEAD_DIM, tk),
                          lambda bi, j: (bi, 0, 0, jnp.minimum(j, n_past - 1)))
    past_v = pl.BlockSpec((nb, tk * ATT_HEADS, 2 * ATT_HEAD_DIM),
                          lambda bi, j: (bi, jnp.minimum(j, n_past - 1), 0))
    new = pl.BlockSpec((nb, tn, ATT_WIDTH), lambda bi, j: (bi, 0, 0))
    return pl.pallas_call(
        functools.partial(_attn_kernel, tq=t, tn=tn, past_len=past_len, out_scale=out_scale),
        grid=(b // nb, n_past + 1),
        in_specs=[
            pl.BlockSpec(memory_space=pltpu.SMEM),
            pl.BlockSpec((nb, t, ATT_WIDTH), lambda bi, j: (bi, 0, 0)),
            past_k, past_v, new, new,
            pl.BlockSpec((1, LANES), lambda bi, j: (0, 0)),
        ],
        out_specs=pl.BlockSpec((nb, t, ATT_WIDTH), lambda bi, j: (bi, 0, 0)),
        out_shape=jax.ShapeDtypeStruct((b, t, ATT_WIDTH), BF16),
        scratch_shapes=[
            pltpu.VMEM((n_items, 2 * t, LANES), BF16),
            pltpu.VMEM((n_items, 2 * t, LANES), F32),
            pltpu.VMEM((n_items, 2 * t, LANES), F32),
            pltpu.VMEM((n_items, 2 * t, LANES), F32),
        ],
        compiler_params=_cparams(("parallel", "arbitrary")),
        name="attention",
    )(lam, q, k_past_t, v_past.reshape(b, past_len * ATT_HEADS, 2 * ATT_HEAD_DIM), k_new, v_new, subln_w)


def _attn_t_kernel(qblk_ref, kblk_ref, lam_ref, q_ref, k_ref, vt_ref, swb_ref, o_ref,
                   qm_ref, m_ref, l_ref, acc_ref, s_ref, bmax_ref, *, tq, tk, past_len, out_scale):
    step = pl.program_id(1)
    i = qblk_ref[step]
    j = kblk_ref[step]
    q0 = past_len + i * tq
    j_last = (((q0 + tq - 1) // CHUNK) * CHUNK + CHUNK - 1) // tk
    col_tile = 2 * LANES
    items = [(s, h) for s in range(q_ref.shape[0]) for h in range(ATT_HEADS)]

    @pl.when(j == 0)
    def _():
        lo_half = lax.broadcasted_iota(jnp.int32, (tq, LANES), 1) < ATT_HEAD_DIM
        for n, (s, h) in enumerate(items):
            qh = q_ref[s, :, h * LANES:(h + 1) * LANES]
            zero = jnp.zeros_like(qh)
            qm_ref[n, 0:tq, :] = jnp.where(lo_half, qh, zero)
            qm_ref[n, tq:2 * tq, :] = jnp.where(lo_half, zero, qh)
        m_ref[...] = jnp.full(m_ref.shape, NEG_BIG, F32)
        l_ref[...] = jnp.zeros(l_ref.shape, F32)
        acc_ref[...] = jnp.zeros(acc_ref.shape, F32)

    def block(masked):
        if masked:
            k_chunk = (j * tk + lax.broadcasted_iota(jnp.int32, (tk, 2 * tq), 0)) >> 6
            q_col = lax.broadcasted_iota(jnp.int32, (tk, 2 * tq), 1) & (tq - 1)
            visible = k_chunk <= ((q0 + q_col) >> 6)

        def scores(n):
            s, h = items[n]
            st = lax.dot_general(k_ref[s, h], qm_ref[n], (((1,), (1,)), ((), ())),
                                 preferred_element_type=F32)
            if masked:
                st = jnp.where(visible, st, NEG_BIG)
            s_ref[n % 2] = st
            bmax_ref[n % 2] = jnp.max(st, axis=0, keepdims=True)

        def update(n):
            s, h = items[n]
            m_prev = m_ref[n]
            m_new = jnp.maximum(m_prev, bmax_ref[n % 2])
            alpha = jnp.exp2(m_prev - m_new)
            m_ref[n] = m_new
            pvs = []
            for c in range(2 * tq // col_tile):
                cs = slice(c * col_tile, (c + 1) * col_tile)
                p = jnp.exp2(s_ref[n % 2, :, cs] - m_new[:, cs])
                pvs.append(jnp.dot(vt_ref[s, h], p.astype(BF16), preferred_element_type=F32))
            pv = jnp.concatenate(pvs, axis=1)
            l_ref[n] = alpha * l_ref[n] + pv[LANES:LANES + 1]
            acc_ref[n] = acc_ref[n] * alpha + pv[0:LANES]

        scores(0)
        for n in range(len(items)):
            if n + 1 < len(items):
                scores(n + 1)
            update(n)

    needs_mask = (j * tk + tk - 1) // CHUNK > q0 // CHUNK
    pl.when(needs_mask)(functools.partial(block, True))
    pl.when(jnp.logical_not(needs_mask))(functools.partial(block, False))

    @pl.when(j == j_last)
    def _():
        lam = lam_ref[0, 0]
        swb = jnp.concatenate([swb_ref[...]] * (tq // LANES), axis=1)
        for n, (s, h) in enumerate(items):
            o = acc_ref[n] / l_ref[n]
            o = o[:, 0:tq] - lam * o[:, tq:2 * tq]
            on = o * lax.rsqrt(jnp.mean(o * o, axis=0, keepdims=True) + NORM_EPS) * swb
            o_ref[s, :, h * LANES:(h + 1) * LANES] = (on * out_scale).T.astype(BF16)


def _attention_t(lam, q, k, vt, subln_w, past_len, out_scale):
    b, t, _ = q.shape
    s_len = k.shape[2]
    tq = min(t, 512)
    tk = min(s_len, 512)
    assert tq & (tq - 1) == 0 and tq >= 2 * LANES and s_len % tk == 0 and t % tq == 0 and tk % LANES == 0
    nq = t // tq
    pairs = [(i, j) for i in range(nq)
             for j in range((((past_len + i * tq + tq - 1) // CHUNK) * CHUNK + CHUNK - 1) // tk + 1)]
    qblk = jnp.asarray([p[0] for p in pairs], jnp.int32)
    kblk = jnp.asarray([p[1] for p in pairs], jnp.int32)

    swb = jnp.broadcast_to(subln_w.reshape(LANES, 1), (LANES, LANES))
    nb = 2 if b % 2 == 0 else 1
    n_items = nb * ATT_HEADS
    grid_spec = pltpu.PrefetchScalarGridSpec(
        num_scalar_prefetch=2,
        grid=(b // nb, len(pairs)),
        in_specs=[
            pl.BlockSpec(memory_space=pltpu.SMEM),
            pl.BlockSpec((nb, tq, ATT_WIDTH), lambda bi, s, qb, kb: (bi, qb[s], 0)),
            pl.BlockSpec((nb, ATT_HEADS, tk, LANES), lambda bi, s, qb, kb: (bi, 0, kb[s], 0)),
            pl.BlockSpec((nb, ATT_HEADS, VT_ROWS, tk), lambda bi, s, qb, kb: (bi, 0, 0, kb[s])),
            pl.BlockSpec((LANES, LANES), lambda bi, s, qb, kb: (0, 0)),
        ],
        out_specs=pl.BlockSpec((nb, tq, ATT_WIDTH), lambda bi, s, qb, kb: (bi, qb[s], 0)),
        scratch_shapes=[
            pltpu.VMEM((n_items, 2 * tq, LANES), BF16),
            pltpu.VMEM((n_items, 1, 2 * tq), F32),
            pltpu.VMEM((n_items, 1, 2 * tq), F32),
            pltpu.VMEM((n_items, LANES, 2 * tq), F32),
            pltpu.VMEM((2, tk, 2 * tq), F32),
            pltpu.VMEM((2, 1, 2 * tq), F32),
        ],
    )
    return pl.pallas_call(
        functools.partial(_attn_t_kernel, tq=tq, tk=tk, past_len=past_len, out_scale=out_scale),
        grid_spec=grid_spec,
        out_shape=jax.ShapeDtypeStruct((b, t, ATT_WIDTH), BF16),
        compiler_params=_cparams(("parallel", "arbitrary")),
        name="attention_t",
    )(qblk, kblk, lam, q, k, vt, swb)


def _ssd_kernel(xbc_ref, z_ref, dt_ref, cw_ref, cb_ref, dtb_ref, alog_ref, dexp_ref, nw_ref, tril_ref,
                cpast_ref, spast_ref, y_ref, st_ref, xpad_ref, xcv_ref, state_ref, *, tt):
    ti = pl.program_id(1)
    L = CHUNK
    hist = SSM_CONV - 1
    base = 8
    assert SSM_CONV == 4

    @pl.when(ti == 0)
    def _():
        xpad_ref[0:base, :] = jnp.zeros((base, SSM_CONV_DIM), F32)
        xpad_ref[base - hist:base, :] = cpast_ref[0]
        state_ref[...] = spast_ref[0]

    xpad_ref[base:base + tt, :] = xbc_ref[0]
    w0, w1, w2, w3 = (cw_ref[d:d + 1, :] for d in range(SSM_CONV))
    for c in range(tt // L):
        r = base + c * L
        xa = xpad_ref[r - 8:r + L, :]
        x1 = pltpu.roll(xa, 1, 0)
        v2 = pltpu.roll(w1 * xa + w0 * x1, 2, 0)
        acc = cb_ref[...] + w3 * xa[8:] + w2 * x1[8:] + v2[8:]
        xcv_ref[c * L:(c + 1) * L, :] = _silu(acc)
    xpad_ref[base - hist:base, :] = xpad_ref[base + tt - hist:base + tt, :]

    lane = lax.broadcasted_iota(jnp.int32, (L, LANES), 1)
    lo_half = lane < SSM_HEAD_DIM
    tril2 = (lane & (SSM_HEAD_DIM - 1)) <= lax.broadcasted_iota(jnp.int32, (L, LANES), 0)
    a_neg = -jnp.exp(alog_ref[...]) * math.log2(math.e)
    tril = tril_ref[...]
    b_off = SSM_D_INNER
    c_off = SSM_D_INNER + SSM_GROUPS * SSM_D_STATE

    def chunk(c, carry):
        r0 = pl.multiple_of(c * L, L)
        rows = pl.ds(r0, L)
        dtv = _softplus(dt_ref[0, rows, :] + dtb_ref[...])
        da = dtv * a_neg
        hi = da.astype(BF16)
        r1 = da - hi.astype(F32)
        mid = r1.astype(BF16)
        lo = (r1 - mid.astype(F32)).astype(BF16)
        acum = (jnp.dot(tril, hi, preferred_element_type=F32)
                + jnp.dot(tril, mid, preferred_element_type=F32)
                + jnp.dot(tril, lo, preferred_element_type=F32))
        acum_t = jnp.concatenate([acum, acum], axis=0).T
        for g in range(SSM_GROUPS):
            gs = slice(g * SSM_GROUP_WIDTH, (g + 1) * SSM_GROUP_WIDTH)
            bm = xcv_ref[rows, b_off + g * SSM_D_STATE:b_off + (g + 1) * SSM_D_STATE].astype(BF16)
            cm = xcv_ref[rows, c_off + g * SSM_D_STATE:c_off + (g + 1) * SSM_D_STATE].astype(BF16)
            cb2 = lax.dot_general(cm, jnp.concatenate([bm, bm], axis=0), (((1,), (1,)), ((), ())),
                                  preferred_element_type=F32)
            st = state_ref[g]
            y_in = jnp.dot(cm, st.astype(BF16), preferred_element_type=F32)
            ys, xws, lasts = [], [], []
            for pp in range(SSM_PAIRS_PER_GROUP):
                pair = g * SSM_PAIRS_PER_GROUP + pp
                h1, h2 = 2 * pair, 2 * pair + 1
                ps = slice(pair * LANES, (pair + 1) * LANES)
                a_col = jnp.where(lo_half, acum[:, h1:h1 + 1], acum[:, h2:h2 + 1])
                dt_col = jnp.where(lo_half, dtv[:, h1:h1 + 1], dtv[:, h2:h2 + 1])
                a_row = jnp.where(lo_half[0:1], acum_t[h1:h1 + 1, :], acum_t[h2:h2 + 1, :])
                last = a_col[L - 1:L, :]
                xs = xcv_ref[rows, ps]
                xdt = xs * dt_col
                w = (cb2 * jnp.exp2(jnp.where(tril2, a_col - a_row, NEG_BIG))).astype(BF16)
                zero = jnp.zeros_like(xdt)
                rhs = jnp.concatenate([jnp.where(lo_half, xdt, zero), jnp.where(lo_half, zero, xdt)],
                                      axis=0).astype(BF16)
                y = (jnp.dot(w, rhs, preferred_element_type=F32)
                     + y_in[:, pp * LANES:(pp + 1) * LANES] * jnp.exp2(a_col)
                     + dexp_ref[:, ps] * xs)
                ys.append(y)
                xws.append(xdt * jnp.exp2(last - a_col))
                lasts.append(last)
            yg = jnp.concatenate(ys, axis=1)
            xwg = jnp.concatenate(xws, axis=1).astype(BF16)
            lastg = jnp.concatenate(lasts, axis=1)
            state_ref[g] = st * jnp.exp2(lastg) + lax.dot_general(
                bm, xwg, (((0,), (0,)), ((), ())), preferred_element_type=F32)
            yg = yg * _silu(z_ref[0, rows, gs])
            yn = yg * lax.rsqrt(jnp.mean(yg * yg, axis=-1, keepdims=True) + NORM_EPS) * nw_ref[:, gs]
            y_ref[0, rows, gs] = yn.astype(BF16)
        return carry

    lax.fori_loop(0, tt // L, chunk, 0, unroll=True)

    @pl.when(ti == pl.num_programs(1) - 1)
    def _():
        st_ref[0] = state_ref[...]


def _ssd(proj3, dt3, conv_past, state_past_t, cw, cb, dtb, alog, dexp, nw, tril):
    b, t, _ = proj3.shape
    tt = min(t, 256)
    full = lambda shape: pl.BlockSpec(shape, lambda bi, i: (0,) * len(shape))
    return pl.pallas_call(
        functools.partial(_ssd_kernel, tt=tt),
        grid=(b, t // tt),
        in_specs=[
            pl.BlockSpec((1, tt, SSM_CONV_DIM), lambda bi, i: (bi, i, COL_XBC // SSM_CONV_DIM)),
            pl.BlockSpec((1, tt, SSM_D_INNER), lambda bi, i: (bi, i, COL_Z // SSM_D_INNER)),
            pl.BlockSpec((1, tt, LANES), lambda bi, i: (bi, i, 0)),
            full((SSM_CONV, SSM_CONV_DIM)), full((1, SSM_CONV_DIM)), full((1, LANES)), full((1, LANES)),
            full((1, SSM_D_INNER)), full((1, SSM_D_INNER)), full((CHUNK, CHUNK)),
            pl.BlockSpec((1, SSM_CONV - 1, SSM_CONV_DIM), lambda bi, i: (bi, 0, 0)),
            pl.BlockSpec((1, SSM_GROUPS, SSM_D_STATE, SSM_GROUP_WIDTH), lambda bi, i: (bi, 0, 0, 0)),
        ],
        out_specs=[
            pl.BlockSpec((1, tt, SSM_D_INNER), lambda bi, i: (bi, i, 0)),
            pl.BlockSpec((1, SSM_GROUPS, SSM_D_STATE, SSM_GROUP_WIDTH), lambda bi, i: (bi, 0, 0, 0)),
        ],
        out_shape=[
            jax.ShapeDtypeStruct((b, t, SSM_D_INNER), BF16),
            jax.ShapeDtypeStruct((b, SSM_GROUPS, SSM_D_STATE, SSM_GROUP_WIDTH), F32),
        ],
        scratch_shapes=[
            pltpu.VMEM((tt + 8, SSM_CONV_DIM), F32),
            pltpu.VMEM((tt, SSM_CONV_DIM), F32),
            pltpu.VMEM((SSM_GROUPS, SSM_D_STATE, SSM_GROUP_WIDTH), F32),
        ],
        compiler_params=_cparams(("parallel", "arbitrary")),
        name="ssd",
    )(proj3, proj3, dt3, cw, cb, dtb, alog, dexp, nw, tril, conv_past, state_past_t)


def _merge_kernel(x_ref, ys_ref, o_ref, gs_ref, ga_ref, wbs_ref, wba_ref, wo_ref, nfw_ref, x1_ref, hn_ref):
    bs = jnp.dot(ys_ref[...], wbs_ref[...], preferred_element_type=F32)
    ba = jnp.dot(o_ref[...], wba_ref[...], preferred_element_type=F32)
    mix = _sigmoid(gs_ref[...]) * bs + _sigmoid(ga_ref[...]) * ba
    x1 = x_ref[...] + jnp.dot(mix.astype(BF16), wo_ref[...], preferred_element_type=F32)
    x1_ref[...] = x1
    hn = x1 * lax.rsqrt(jnp.mean(x1 * x1, axis=-1, keepdims=True) + NORM_EPS) * nfw_ref[...]
    hn_ref[...] = hn.astype(BF16)


def _merge(x2d, ys, o, proj, wbs, wba, wo, nfw):
    n = x2d.shape[0]
    tm = min(n, 512)
    row = lambda w, c=0: pl.BlockSpec((tm, w), lambda i: (i, c))
    full = lambda shape: pl.BlockSpec(shape, lambda i: (0, 0))
    return pl.pallas_call(
        _merge_kernel,
        grid=(n // tm,),
        in_specs=[row(D_MODEL), row(SSM_D_INNER), row(ATT_WIDTH),
                  row(D_MODEL, COL_GS // D_MODEL), row(D_MODEL, COL_GA // D_MODEL),
                  full((SSM_D_INNER, D_MODEL)), full((ATT_WIDTH, D_MODEL)), full((D_MODEL, D_MODEL)),
                  full((1, D_MODEL))],
        out_specs=[row(D_MODEL), row(D_MODEL)],
        out_shape=[jax.ShapeDtypeStruct((n, D_MODEL), F32), jax.ShapeDtypeStruct((n, D_MODEL), BF16)],
        compiler_params=_cparams(("parallel",)),
        name="merge",
    )(x2d, ys, o, proj, proj, wbs, wba, wo, nfw)


FFN_SPLIT = 2


def _ffn_kernel(hn_ref, x1_ref, wa_ref, wb_ref, wd_ref, cw_ref, cb_ref, past_ref, out_ref, tail_ref,
                hbuf_ref, *, tm):
    ti = pl.program_id(1)
    hist = FFN_CONV - 1
    base = 8
    tf = D_FF // FFN_SPLIT

    @pl.when(ti == 0)
    def _():
        hbuf_ref[base - hist:base, :] = past_ref[0]

    hn = hn_ref[0]
    acc = x1_ref[0]
    for f in range(FFN_SPLIT):
        fs = slice(f * tf, (f + 1) * tf)
        ha = jnp.dot(hn, wa_ref[:, fs], preferred_element_type=F32)
        hb = jnp.dot(hn, wb_ref[:, fs], preferred_element_type=F32)
        hbuf_ref[base:base + tm, fs] = ha
        hc = cb_ref[:, fs] + cw_ref[hist:hist + 1, fs] * ha
        for d in range(1, FFN_CONV):
            hc = hc + cw_ref[hist - d:hist - d + 1, fs] * hbuf_ref[base - d:base - d + tm, fs]
        u = (_silu(hc) * hb).astype(BF16)
        acc = acc + jnp.dot(u, wd_ref[fs, :], preferred_element_type=F32)
    out_ref[0] = acc
    tail = hbuf_ref[base + tm - hist:base + tm, :]
    hbuf_ref[base - hist:base, :] = tail
    tail_ref[0] = tail


def _ffn(hn3, x13, wa, wb, wd, cw, cb, past):
    b, t, _ = hn3.shape
    tm = min(t, 512)
    const = lambda shape: pl.BlockSpec(shape, lambda bi, i: (0, 0), pipeline_mode=pl.Buffered(1))
    return pl.pallas_call(
        functools.partial(_ffn_kernel, tm=tm),
        grid=(b, t // tm),
        in_specs=[
            pl.BlockSpec((1, tm, D_MODEL), lambda bi, i: (bi, i, 0)),
            pl.BlockSpec((1, tm, D_MODEL), lambda bi, i: (bi, i, 0)),
            const((D_MODEL, D_FF)), const((D_MODEL, D_FF)), const((D_FF, D_MODEL)),
            const((FFN_CONV, D_FF)), const((1, D_FF)),
            pl.BlockSpec((1, FFN_CONV - 1, D_FF), lambda bi, i: (bi, 0, 0)),
        ],
        out_specs=[
            pl.BlockSpec((1, tm, D_MODEL), lambda bi, i: (bi, i, 0)),
            pl.BlockSpec((1, FFN_CONV - 1, D_FF), lambda bi, i: (bi, 0, 0)),
        ],
        out_shape=[
            jax.ShapeDtypeStruct((b, t, D_MODEL), F32),
            jax.ShapeDtypeStruct((b, FFN_CONV - 1, D_FF), F32),
        ],
        scratch_shapes=[pltpu.VMEM((tm + 8, D_FF), F32)],
        compiler_params=_cparams(("parallel", "arbitrary")),
        name="ffn",
    )(hn3, x13, wa, wb, wd, cw, cb, past)


def _prepare_weights(norm_mix_w, w_in, ssm_conv_w, ssm_conv_b, ssm_dt_bias, ssm_a_log, ssm_d, ssm_norm_w,
                     q_norm_w, k_norm_w, subln_w, w_branch_ssm, w_branch_attn, w_out, norm_ffn_w, w_up,
                     ffn_conv_w, ffn_conv_b, w_down):
    bounds = [0]
    for s in IN_SIZES:
        bounds.append(bounds[-1] + s)
    piece = lambda n: w_in[:, bounds[n]:bounds[n + 1]]
    z, xbc, dt, q, k, v, gs, ga = (piece(n) for n in range(8))
    pad_heads = lambda a: jnp.pad(a.reshape(1, SSM_HEADS), ((0, 0), (0, LANES - SSM_HEADS)))
    row = lambda a: a.reshape(1, -1)
    two_heads = lambda a: jnp.concatenate([a, a]).reshape(1, LANES)
    head = jnp.arange(LANES) // ATT_HEAD_DIM
    tri = jnp.arange(CHUNK)
    return dict(
        norm_mix=row(norm_mix_w),
        w_main=jnp.concatenate([z, q, k, v, gs, xbc, ga], axis=1).astype(BF16),
        w_dt=jnp.pad(dt, ((0, 0), (0, LANES - SSM_HEADS))).astype(BF16),
        conv_w=ssm_conv_w, conv_b=row(ssm_conv_b),
        dt_bias=pad_heads(ssm_dt_bias), a_log=pad_heads(ssm_a_log),
        d_exp=row(jnp.repeat(ssm_d, SSM_HEAD_DIM)), ssm_norm=row(ssm_norm_w),
        tril=(tri[:, None] >= tri[None, :]).astype(BF16),
        q_norm=two_heads(q_norm_w), k_norm=two_heads(k_norm_w),
        head_ones=(head[:, None] == head[None, :]).astype(BF16),
        subln=row(subln_w),
        w_bs=w_branch_ssm.astype(BF16), w_ba=w_branch_attn.astype(BF16), w_out=w_out.astype(BF16),
        norm_ffn=row(norm_ffn_w),
        w_up_a=w_up[:, :D_FF].astype(BF16), w_up_b=w_up[:, D_FF:].astype(BF16),
        w_down=w_down.astype(BF16), ffn_conv_w=ffn_conv_w, ffn_conv_b=row(ffn_conv_b),
    )


def _state_to_kernel_layout(s):
    b = s.shape[0]
    s = s.reshape(b, SSM_GROUPS, SSM_HEADS // SSM_GROUPS, SSM_HEAD_DIM, SSM_D_STATE)
    return jnp.transpose(s, (0, 1, 4, 2, 3)).reshape(b, SSM_GROUPS, SSM_D_STATE, SSM_GROUP_WIDTH)


def _state_from_kernel_layout(s):
    b = s.shape[0]
    s = s.reshape(b, SSM_GROUPS, SSM_D_STATE, SSM_HEADS // SSM_GROUPS, SSM_HEAD_DIM)
    return jnp.transpose(s, (0, 1, 3, 4, 2)).reshape(b, SSM_HEADS, SSM_HEAD_DIM, SSM_D_STATE)


def _layer(x, k_past, v_past, ssm_past, conv_past, ffn_past, lam, lambda_init, w):
    b, t, _ = x.shape
    past_len = k_past.shape[1]
    n = b * t
    x2d = x.reshape(n, D_MODEL)

    proj, dt = _in_proj(x2d, w["norm_mix"], w["w_main"], w["w_dt"])
    keys_on_rows = past_len == 0 and t % (4 * LANES) == 0
    qb, kf, kb, vf, vb = _qk_prep(proj, _rope_tables(t, past_len), w["q_norm"], w["k_norm"],
                                  w["head_ones"], t, keys_on_rows)
    q3 = qb.reshape(b, t, ATT_WIDTH)
    if keys_on_rows:
        o = _attention_t(lam, q3, kb, vb, w["subln"], past_len, 1.0 - lambda_init)
    else:
        o = _attention(lam, q3, jnp.transpose(k_past, (0, 2, 3, 1)), v_past, kb.reshape(b, t, ATT_WIDTH),
                       vb.reshape(b, t, ATT_WIDTH), w["subln"], 1.0 - lambda_init)

    ys, state_t = _ssd(proj.reshape(b, t, PROJ_WIDTH), dt.reshape(b, t, LANES), conv_past,
                       _state_to_kernel_layout(ssm_past), w["conv_w"], w["conv_b"], w["dt_bias"],
                       w["a_log"], w["d_exp"], w["ssm_norm"], w["tril"])

    x1, hn = _merge(x2d, ys.reshape(n, SSM_D_INNER), o.reshape(n, ATT_WIDTH), proj,
                    w["w_bs"], w["w_ba"], w["w_out"], w["norm_ffn"])
    y, ffn_new = _ffn(hn.reshape(b, t, D_MODEL), x1.reshape(b, t, D_MODEL), w["w_up_a"], w["w_up_b"],
                      w["w_down"], w["ffn_conv_w"], w["ffn_conv_b"], ffn_past)

    xbc_rows = proj.reshape(b, t, PROJ_WIDTH)[:, t - (SSM_CONV - 1):, COL_XBC:COL_XBC + SSM_CONV_DIM]
    k_new = jnp.transpose(kf, (0, 3, 1, 2))
    v_new = vf.reshape(b, t, ATT_HEADS, 2 * ATT_HEAD_DIM)
    return y, k_new, v_new, _state_from_kernel_layout(state_t), xbc_rows, ffn_new


def kernel(x_prompt, x_sample, cache_k, cache_v, state_ssm, state_ssm_conv, state_ffn_conv, norm_mix_w, w_in, ssm_conv_w, ssm_conv_b, ssm_dt_bias, ssm_a_log, ssm_d, ssm_norm_w, q_norm_w, k_norm_w, lambda_q1, lambda_k1, lambda_q2, lambda_k2, subln_w, w_branch_ssm, w_branch_attn, w_out, norm_ffn_w, w_up, ffn_conv_w, ffn_conv_b, w_down):
    depth = w_in.shape[0]
    assert depth == 1
    bp = x_prompt.shape[0]
    dt_ = x_prompt.dtype
    layer = 0
    lambda_init = 0.8 - 0.6 * math.exp(-0.3 * layer)
    w = _prepare_weights(norm_mix_w[layer], w_in[layer], ssm_conv_w[layer], ssm_conv_b[layer],
                         ssm_dt_bias[layer], ssm_a_log[layer], ssm_d[layer], ssm_norm_w[layer],
                         q_norm_w[layer], k_norm_w[layer], subln_w[layer], w_branch_ssm[layer],
                         w_branch_attn[layer], w_out[layer], norm_ffn_w[layer], w_up[layer],
                         ffn_conv_w[layer], ffn_conv_b[layer], w_down[layer])
    lam = _lambda(lambda_q1[layer].reshape(1, -1), lambda_k1[layer].reshape(1, -1),
                  lambda_q2[layer].reshape(1, -1), lambda_k2[layer].reshape(1, -1), lambda_init)

    yp, kp, vp, sp, cp, fp = _layer(
        x_prompt,
        jnp.zeros((bp, 0, 2 * ATT_HEADS, ATT_HEAD_DIM), dt_),
        jnp.zeros((bp, 0, ATT_HEADS, 2 * ATT_HEAD_DIM), dt_),
        jnp.zeros((bp, SSM_HEADS, SSM_HEAD_DIM, SSM_D_STATE), dt_),
        jnp.zeros((bp, SSM_CONV - 1, SSM_CONV_DIM), dt_),
        jnp.zeros((bp, FFN_CONV - 1, D_FF), dt_),
        lam, lambda_init, w)
    ys, ks, vs, ss, cs, fs = _layer(
        x_sample, cache_k[layer], cache_v[layer], state_ssm[layer], state_ssm_conv[layer],
        state_ffn_conv[layer], lam, lambda_init, w)
    stack = lambda a: a[None]
    return (yp, ys, stack(kp), stack(vp), stack(sp), stack(cp), stack(fp),
            stack(ks), stack(vs), stack(ss), stack(cs), stack(fs))
```
